```python
import math
import jax
import jax.numpy as jnp
from jax import lax
import numpy as np

D_MODEL = 1024
BATCH = 4
SEQ = 4096
DEPTH = 4

GRID_W = 64
CTX_LEN = 256
EPS = 1e-6
ROPE_BASE = 10000.0
QBLK = 128

A_HEADS = 4
A_KV_HEADS = 2
A_HEAD_DIM = 64
A_WINDOW = 128
B_D_INNER = 512
B_HEADDIM = 64
B_HEADS = B_D_INNER // B_HEADDIM
B_GROUPS = 2
B_STATE = 64
B_CONV = 5
B_CHUNK = 128
C_HEADS = 4
C_QK_DIM = 32
C_V_DIM = 64

A_WIDTH = A_HEADS * A_HEAD_DIM
C_WIDTH = C_HEADS * C_V_DIM
MIX_WIDTH = A_WIDTH + B_D_INNER + C_WIDTH

A_Q = A_HEADS * A_HEAD_DIM
A_KV = A_KV_HEADS * A_HEAD_DIM
B_XBC = B_D_INNER + 2 * B_GROUPS * B_STATE
B_DT = 2 * B_HEADS
C_QK = C_HEADS * 2 * C_QK_DIM
OFF_AK = A_Q
OFF_AV = OFF_AK + A_KV
OFF_BZ = OFF_AV + A_KV
OFF_BX = OFF_BZ + B_D_INNER
OFF_BDT = OFF_BX + B_XBC
OFF_CQ = OFF_BDT + B_DT
OFF_CK = OFF_CQ + C_QK
OFF_CV = OFF_CK + C_QK
IN_COLS = OFF_CV + C_WIDTH
IN_SPLITS = (OFF_AK, OFF_AV, OFF_BZ, OFF_BX, OFF_BDT, OFF_CQ, OFF_CK, OFF_CV)

N_GROUPS = 4
EXPERTS_PER_GROUP = 8
N_EXPERTS = N_GROUPS * EXPERTS_PER_GROUP
TOP_K = 2
D_EXPERT = 512
MOE_BLK = 256

kernel_name = 'hybrid_flow_backbone'

F32 = jnp.float32


def rmsnorm(x, w):
    xf = x.astype(F32)
    y = xf * lax.rsqrt(jnp.mean(xf * xf, axis=-1, keepdims=True) + EPS)
    return (y * w.astype(F32)).astype(x.dtype)


def modulate(h, shift, scale):
    return h * (1 + scale) + shift


def axial_rope(x, row_pos, col_pos):
    dim = x.shape[-1]
    half = dim // 2
    nf = half // 2
    freqs = ROPE_BASE ** (-jnp.arange(nf, dtype=F32) / nf)

    def rot(u, pos):
        ang = pos.astype(F32)[:, None] * freqs
        cos = jnp.cos(ang)[None, :, None, :].astype(u.dtype)
        sin = jnp.sin(ang)[None, :, None, :].astype(u.dtype)
        u1, u2 = u[..., :nf], u[..., nf:]
        return jnp.concatenate([u1 * cos - u2 * sin, u1 * sin + u2 * cos], axis=-1)

    return jnp.concatenate([rot(x[..., :half], row_pos), rot(x[..., half:], col_pos)], axis=-1)


def window_attention_sink(q, k, v, k_ctx, v_ctx, sink):
    Bsz, L, H, d = q.shape
    G = H // A_KV_HEADS
    nb = L // QBLK
    Lc = k_ctx.shape[1]
    qb = q.reshape(Bsz, nb, QBLK, A_KV_HEADS, G, d)

    def band(u):
        up = jnp.pad(u, ((0, 0), (QBLK, QBLK), (0, 0), (0, 0))).reshape(Bsz, nb + 2, QBLK, A_KV_HEADS, d)
        return jnp.concatenate([up[:, :-2], up[:, 1:-1], up[:, 2:]], axis=2)

    kb, vb = band(k), band(v)
    scale = d ** -0.5
    s_loc = jnp.einsum('bnqhgd,bnkhd->bnhgqk', qb, kb).astype(F32) * scale
    s_ctx = jnp.einsum('bnqhgd,bkhd->bnhgqk', qb, k_ctx).astype(F32) * scale
    blk = jnp.arange(nb)[:, None, None]
    qpos = blk * QBLK + jnp.arange(QBLK)[None, :, None]
    kpos = (blk - 1) * QBLK + jnp.arange(3 * QBLK)[None, None, :]
    mask = (jnp.abs(kpos - qpos) <= A_WINDOW) & (kpos >= 0) & (kpos < L)
    s_loc = jnp.where(mask[None, :, None, None], s_loc, -jnp.inf)
    s_sink = jnp.broadcast_to(sink.astype(F32).reshape(1, 1, A_KV_HEADS, G, 1, 1), s_loc.shape[:-1] + (1,))
    p = jax.nn.softmax(jnp.concatenate([s_loc, s_ctx, s_sink], axis=-1), axis=-1).astype(v.dtype)
    nk = 3 * QBLK
    out = (jnp.einsum('bnhgqk,bnkhd->bnqhgd', p[..., :nk], vb)
           + jnp.einsum('bnhgqk,bkhd->bnqhgd', p[..., nk:nk + Lc], v_ctx))
    return out.reshape(Bsz, L, H * d)


def ctx_attention_sink(q, k, v, sink):
    Bsz, Lq, H, d = q.shape
    G = H // A_KV_HEADS
    qg = q.reshape(Bsz, Lq, A_KV_HEADS, G, d)
    s = jnp.einsum('bqhgd,bkhd->bhgqk', qg, k).astype(F32) * d ** -0.5
    s_sink = jnp.broadcast_to(sink.astype(F32).reshape(1, A_KV_HEADS, G, 1, 1), s.shape[:-1] + (1,))
    p = jax.nn.softmax(jnp.concatenate([s, s_sink], axis=-1), axis=-1)[..., :-1].astype(v.dtype)
    return jnp.einsum('bhgqk,bkhd->bqhgd', p, v).reshape(Bsz, Lq, H * d)


def depthwise_conv_centred(u, w, b):
    out = lax.conv_general_dilated(u, w[:, None, :].astype(u.dtype), window_strides=(1,),
                                   padding=[(B_CONV // 2, B_CONV // 2)],
                                   dimension_numbers=('NWC', 'WIO', 'NWC'),
                                   feature_group_count=u.shape[-1])
    return out + b.astype(u.dtype)


def ssd_chunked(x, dt, a, bm, cm, h0, want_y):
    Bsz, L, H, P = x.shape
    nc = L // B_CHUNK
    rep = H // B_GROUPS
    bh = jnp.repeat(bm, rep, axis=2).astype(F32).reshape(Bsz, nc, B_CHUNK, H, B_STATE)
    xdt = (x.astype(F32) * dt[..., None]).reshape(Bsz, nc, B_CHUNK, H, P)
    a_cum = jnp.cumsum((dt * a).reshape(Bsz, nc, B_CHUNK, H), axis=2)
    a_end = a_cum[:, :, -1]
    states = jnp.einsum('bclhn,bclh,bclhp->bchpn', bh, jnp.exp(a_end[:, :, None] - a_cum), xdt)

    def step(h, inp):
        dec, st = inp
        return dec[..., None, None] * h + st, h

    h_final, h_start = lax.scan(step, h0, (jnp.moveaxis(jnp.exp(a_end), 1, 0), jnp.moveaxis(states, 1, 0)))
    if not want_y:
        return None, h_final
    h_start = jnp.moveaxis(h_start, 0, 1)
    ch = jnp.repeat(cm, rep, axis=2).astype(F32).reshape(Bsz, nc, B_CHUNK, H, B_STATE)
    tri = jnp.tril(jnp.ones((B_CHUNK, B_CHUNK), bool))[None, None, :, :, None]
    seg = a_cum[:, :, :, None, :] - a_cum[:, :, None, :, :]
    decay = jnp.where(tri, jnp.exp(jnp.where(tri, seg, 0.0)), 0.0)
    scores = jnp.einsum('bclhn,bcshn->bclsh', ch, bh) * decay
    y_diag = jnp.einsum('bclsh,bcshp->bclhp', scores, xdt)
    y_off = jnp.einsum('bclhn,bchpn,bclh->bclhp', ch, h_start, jnp.exp(a_cum))
    return (y_diag + y_off).reshape(Bsz, L, H, P), h_final


def ssd_bidirectional(z, xbc, dt_raw, z_ctx, xbc_ctx, dt_raw_ctx, conv_w, conv_b, dt_bias, a_log,
                      d_skip, norm_w, need_ctx):
    a = -jnp.exp(a_log.astype(F32))

    def prep(xbc_, dt_raw_):
        Bsz, L, _ = xbc_.shape
        u = jax.nn.silu(depthwise_conv_centred(xbc_, conv_w, conv_b))
        xs, bm, cm = jnp.split(u, [B_D_INNER, B_D_INNER + B_GROUPS * B_STATE], axis=-1)
        xs = xs.reshape(Bsz, L, B_HEADS, B_HEADDIM)
        bm = bm.reshape(Bsz, L, B_GROUPS, B_STATE)
        cm = cm.reshape(Bsz, L, B_GROUPS, B_STATE)
        dt = jax.nn.softplus(dt_raw_.astype(F32).reshape(Bsz, L, 2, B_HEADS) + dt_bias.astype(F32))
        return xs, bm, cm, dt

    def scan_dir(xs, bm, cm, dt, dirn, h0, want_y):
        dt_d = dt[:, :, dirn]
        if dirn == 1:
            xs, bm, cm, dt_d = (jnp.flip(xs, 1), jnp.flip(bm, 1), jnp.flip(cm, 1), jnp.flip(dt_d, 1))
        y, h_t = ssd_chunked(xs, dt_d, a[dirn], bm, cm, h0, want_y)
        if dirn == 1 and y is not None:
            y = jnp.flip(y, 1)
        return y, h_t

    def finish(y_f, y_b, xs, z_):
        Bsz, L = xs.shape[:2]
        y = y_f + y_b + xs.astype(F32) * (d_skip[0] + d_skip[1]).astype(F32)[:, None]
        g = y.reshape(Bsz, L, B_D_INNER) * jax.nn.silu(z_.astype(F32))
        g = g.reshape(Bsz, L, B_GROUPS, B_D_INNER // B_GROUPS)
        g = g * lax.rsqrt(jnp.mean(g * g, axis=-1, keepdims=True) + EPS)
        return (g.reshape(Bsz, L, B_D_INNER) * norm_w.astype(F32)).astype(z_.dtype)

    xs_c, bm_c, cm_c, dt_c = prep(xbc_ctx, dt_raw_ctx)
    h0 = jnp.zeros((xs_c.shape[0], B_HEADS, B_HEADDIM, B_STATE), F32)
    yc_f, hc_f = scan_dir(xs_c, bm_c, cm_c, dt_c, 0, h0, need_ctx)
    yc_b, hc_b = scan_dir(xs_c, bm_c, cm_c, dt_c, 1, h0, need_ctx)
    xs, bm, cm, dt = prep(xbc, dt_raw)
    y_f, _ = scan_dir(xs, bm, cm, dt, 0, hc_f, True)
    y_b, _ = scan_dir(xs, bm, cm, dt, 1, hc_b, True)
    out = finish(y_f, y_b, xs, z)
    out_ctx = finish(yc_f, yc_b, xs_c, z_ctx) if need_ctx else None
    return out, out_ctx


def diff_attend(q, k, v, lam):
    s = jnp.einsum('bqhmd,bkhmd->bhmqk', q, k).astype(F32) * q.shape[-1] ** -0.5
    p = jax.nn.softmax(s, axis=-1)
    w = (p[:, :, 0] - lam * p[:, :, 1]).astype(v.dtype)
    return jnp.einsum('bhqk,bkhd->bqhd', w, v)


def diff_attention_latent(q, k, v, k_ctx, v_ctx, lam):
    Bsz, L = q.shape[:2]
    nb = L // QBLK
    k_all = jnp.concatenate([k, k_ctx], axis=1)
    v_all = jnp.concatenate([v, v_ctx], axis=1)
    qb = jnp.moveaxis(q.reshape(Bsz, nb, QBLK, C_HEADS, 2, C_QK_DIM), 1, 0)
    out = lax.map(lambda qblk: diff_attend(qblk, k_all, v_all, lam), qb)
    return jnp.moveaxis(out, 0, 1).reshape(Bsz, L, C_HEADS, C_V_DIM)


def diff_post(o, subln_w, lambda_init):
    return (rmsnorm(o, subln_w) * (1.0 - lambda_init)).reshape(o.shape[0], o.shape[1], C_WIDTH)


def token_mixing(h, h_ctx, w_in, w_out, a_sink, conv_w, conv_b, dt_bias, a_log, d_skip, b_norm_w,
                 c_lambda, c_subln_w, lambda_init, row_pos, col_pos, need_ctx):
    Bsz, L, _ = h.shape
    Lc = h_ctx.shape[1]
    aq, ak, av, bz, bx, bdt, cq, ck, cv = jnp.split(jnp.dot(h, w_in), IN_SPLITS, axis=-1)
    aq_c, ak_c, av_c, bz_c, bx_c, bdt_c, cq_c, ck_c, cv_c = jnp.split(jnp.dot(h_ctx, w_in), IN_SPLITS, axis=-1)
    q = axial_rope(aq.reshape(Bsz, L, A_HEADS, A_HEAD_DIM), row_pos, col_pos)
    k = axial_rope(ak.reshape(Bsz, L, A_KV_HEADS, A_HEAD_DIM), row_pos, col_pos)
    v = av.reshape(Bsz, L, A_KV_HEADS, A_HEAD_DIM)
    k_c = ak_c.reshape(Bsz, Lc, A_KV_HEADS, A_HEAD_DIM)
    v_c = av_c.reshape(Bsz, Lc, A_KV_HEADS, A_HEAD_DIM)
    out_a = window_attention_sink(q, k, v, k_c, v_c, a_sink)
    out_b, out_b_c = ssd_bidirectional(bz, bx, bdt, bz_c, bx_c, bdt_c, conv_w, conv_b, dt_bias, a_log,
                                       d_skip, b_norm_w, need_ctx)
    lq1, lk1, lq2, lk2 = c_lambda.astype(F32)
    lam = jnp.exp(jnp.sum(lq1 * lk1)) - jnp.exp(jnp.sum(lq2 * lk2)) + lambda_init
    qd = axial_rope(cq.reshape(Bsz, L, 2 * C_HEADS, C_QK_DIM), row_pos, col_pos).reshape(Bsz, L, C_HEADS, 2, C_QK_DIM)
    kd = axial_rope(ck.reshape(Bsz, L, 2 * C_HEADS, C_QK_DIM), row_pos, col_pos).reshape(Bsz, L, C_HEADS, 2, C_QK_DIM)
    vd = cv.reshape(Bsz, L, C_HEADS, C_V_DIM)
    kd_c = ck_c.reshape(Bsz, Lc, C_HEADS, 2, C_QK_DIM)
    vd_c = cv_c.reshape(Bsz, Lc, C_HEADS, C_V_DIM)
    out_c = diff_post(diff_attention_latent(qd, kd, vd, kd_c, vd_c, lam), c_subln_w, lambda_init)
    out = jnp.dot(jnp.concatenate([out_a, out_b, out_c], axis=-1), w_out)
    if not need_ctx:
        return out, None
    out_a_c = ctx_attention_sink(aq_c.reshape(Bsz, Lc, A_HEADS, A_HEAD_DIM), k_c, v_c, a_sink)
    qd_c = cq_c.reshape(Bsz, Lc, C_HEADS, 2, C_QK_DIM)
    out_c_c = diff_post(diff_attend(qd_c, kd_c, vd_c, lam), c_subln_w, lambda_init)
    out_ctx = jnp.dot(jnp.concatenate([out_a_c, out_b_c, out_c_c], axis=-1), w_out)
    return out, out_ctx


def grouped_expert_mlp(t, e_idx, gate, w_gate, w_up, w_down):
    N, D = t.shape
    A = N * TOP_K
    flat_e = e_idx.reshape(-1)
    order = jnp.argsort(flat_e)
    sorted_e = flat_e[order]
    counts = jnp.bincount(flat_e, length=N_EXPERTS)
    padded = (counts + MOE_BLK - 1) // MOE_BLK * MOE_BLK
    pad_end = jnp.cumsum(padded)
    pad_start = pad_end - padded
    start = jnp.cumsum(counts) - counts
    dest = pad_start[sorted_e] + jnp.arange(A) - start[sorted_e]
    n_blocks = -(-A // MOE_BLK) + N_EXPERTS
    tok = order // TOP_K
    buf = jnp.zeros((n_blocks * MOE_BLK, D), t.dtype).at[dest].set(t[tok])
    block_e = jnp.minimum(jnp.searchsorted(pad_end, jnp.arange(n_blocks) * MOE_BLK, side='right'), N_EXPERTS - 1)

    def run(args):
        xb, e = args
        hid = jax.nn.silu(jnp.dot(xb, w_gate[e])) * jnp.dot(xb, w_up[e])
        return jnp.dot(hid, w_down[e])

    out = lax.map(run, (buf.reshape(n_blocks, MOE_BLK, D), block_e)).reshape(-1, D)
    w_sorted = gate.reshape(-1)[order].astype(t.dtype)
    return jnp.zeros_like(t).at[tok].add(out[dest] * w_sorted[:, None])


def hier_moe(t, w_group_router, w_router, w_gate, w_up, w_down):
    N = t.shape[0]
    g_prob = jax.nn.softmax(jnp.dot(t, w_group_router).astype(F32), axis=-1)
    g_p, g_sel = lax.top_k(g_prob, 1)
    e_logits = jnp.dot(t, w_router).astype(F32).reshape(N, N_GROUPS, EXPERTS_PER_GROUP)
    in_grp = jnp.take_along_axis(e_logits, g_sel[:, :, None], axis=1)[:, 0]
    e_val, e_loc = lax.top_k(in_grp, TOP_K)
    gate = g_p * jax.nn.softmax(e_val, axis=-1)
    e_idx = g_sel * EXPERTS_PER_GROUP + e_loc
    return grouped_expert_mlp(t, e_idx, gate, w_gate, w_up, w_down)


def setup_inputs(seed: int = 0) -> dict:
    key = jax.random.key(seed)
    ks = jax.random.split(key, 26)

    def nrm(k, shape, scale):
        return jax.random.normal(k, shape, F32) * scale

    dt0 = jnp.exp(jax.random.uniform(ks[12], (DEPTH, 2, B_HEADS), F32, math.log(1e-3), math.log(1e-1)))
    return {
        'x': nrm(ks[0], (BATCH, SEQ, D_MODEL), 1.0),
        'c': nrm(ks[1], (BATCH, D_MODEL), 1.0),
        'ctx': nrm(ks[2], (BATCH, CTX_LEN, D_MODEL), 1.0),
        'c_ctx': nrm(ks[3], (D_MODEL,), 1.0),
        'w_mod': nrm(ks[4], (DEPTH, D_MODEL, 6 * D_MODEL), 0.3 * D_MODEL ** -0.5),
        'b_mod': nrm(ks[5], (DEPTH, 6 * D_MODEL), 0.01),
        'norm1_w': 1.0 + nrm(ks[6], (DEPTH, D_MODEL), 0.02),
        'norm2_w': 1.0 + nrm(ks[7], (DEPTH, D_MODEL), 0.02),
        'w_in': nrm(ks[8], (DEPTH, D_MODEL, IN_COLS), D_MODEL ** -0.5),
        'w_out': nrm(ks[9], (DEPTH, MIX_WIDTH, D_MODEL), MIX_WIDTH ** -0.5),
        'a_sink': nrm(ks[10], (DEPTH, A_HEADS), 0.5),
        'b_conv_w': nrm(ks[11], (DEPTH, B_CONV, B_XBC), B_CONV ** -0.5),
        'b_conv_b': nrm(ks[13], (DEPTH, B_XBC), 0.01),
        'b_dt_bias': dt0 + jnp.log(-jnp.expm1(-dt0)),
        'b_a_log': jnp.log(jax.random.uniform(ks[14], (DEPTH, 2, B_HEADS), F32, 1.0, 16.0)),
        'b_d': 0.5 + nrm(ks[15], (DEPTH, 2, B_HEADS), 0.1),
        'b_norm_w': 1.0 + nrm(ks[16], (DEPTH, B_D_INNER), 0.02),
        'c_lambda': nrm(ks[17], (DEPTH, 4, C_QK_DIM), 0.1),
        'c_subln_w': 1.0 + nrm(ks[18], (DEPTH, C_V_DIM), 0.02),
        'moe_group_router': nrm(ks[19], (DEPTH, D_MODEL, N_GROUPS), D_MODEL ** -0.5),
        'moe_router': nrm(ks[20], (DEPTH, D_MODEL, N_EXPERTS), D_MODEL ** -0.5),
        'moe_w_gate': nrm(ks[21], (DEPTH, N_EXPERTS, D_MODEL, D_EXPERT), D_MODEL ** -0.5),
        'moe_w_up': nrm(ks[22], (DEPTH, N_EXPERTS, D_MODEL, D_EXPERT), D_MODEL ** -0.5),
        'moe_w_down': nrm(ks[23], (DEPTH, N_EXPERTS, D_EXPERT, D_MODEL), D_EXPERT ** -0.5),
        'final_norm_w': 1.0 + nrm(ks[24], (D_MODEL,), 0.02),
    }


def reference(x, c, ctx, c_ctx, w_mod, b_mod, norm1_w, norm2_w, w_in, w_out, a_sink, b_conv_w, b_conv_b,
              b_dt_bias, b_a_log, b_d, b_norm_w, c_lambda, c_subln_w, moe_group_router, moe_router,
              moe_w_gate, moe_w_up, moe_w_down, final_norm_w):
    Bsz, L, D = x.shape
    rows = L // GRID_W
    row_pos = jnp.repeat(jnp.arange(rows, dtype=jnp.int32), GRID_W)
    col_pos = jnp.tile(jnp.arange(GRID_W, dtype=jnp.int32), rows)
    x_ctx = ctx
    for l in range(DEPTH):
        last = l == DEPTH - 1
        mod = jnp.dot(jax.nn.silu(c), w_mod[l]) + b_mod[l]
        mod_ctx = jnp.dot(jax.nn.silu(c_ctx), w_mod[l]) + b_mod[l]
        sh1, sc1, g1, sh2, sc2, g2 = jnp.split(mod[:, None, :], 6, axis=-1)
        sh1c, sc1c, g1c, sh2c, sc2c, g2c = jnp.split(mod_ctx, 6, axis=-1)
        h = modulate(rmsnorm(x, norm1_w[l]), sh1, sc1)
        h_ctx = modulate(rmsnorm(x_ctx, norm1_w[l]), sh1c, sc1c)
        lambda_init = 0.8 - 0.6 * math.exp(-0.3 * l)
        mix, mix_ctx = token_mixing(h, h_ctx, w_in[l], w_out[l], a_sink[l], b_conv_w[l], b_conv_b[l],
                                    b_dt_bias[l], b_a_log[l], b_d[l], b_norm_w[l], c_lambda[l], c_subln_w[l],
                                    lambda_init, row_pos, col_pos, not last)
        x = x + g1 * mix
        h2 = modulate(rmsnorm(x, norm2_w[l]), sh2, sc2).reshape(-1, D)
        if last:
            y = hier_moe(h2, moe_group_router[l], moe_router[l], moe_w_gate[l], moe_w_up[l], moe_w_down[l])
            x = x + g2 * y.reshape(Bsz, L, D)
        else:
            x_ctx = x_ctx + g1c * mix_ctx
            h2c = modulate(rmsnorm(x_ctx, norm2_w[l]), sh2c, sc2c).reshape(-1, D)
            y = hier_moe(jnp.concatenate([h2, h2c], axis=0), moe_group_router[l], moe_router[l],
                         moe_w_gate[l], moe_w_up[l], moe_w_down[l])
            x = x + g2 * y[:Bsz * L].reshape(Bsz, L, D)
            x_ctx = x_ctx + g2c * y[Bsz * L:].reshape(Bsz, -1, D)
    return rmsnorm(x, final_norm_w)
```

```python
import functools
import math

import jax
import jax.numpy as jnp
from jax import lax
from jax.experimental import pallas as pl
from jax.experimental.pallas import tpu as pltpu

F32 = jnp.float32
BF16 = jnp.bfloat16
HI = lax.Precision.HIGHEST

GRID_W = 64
EPS = 1e-6
ROPE_BASE = 10000.0

A_HEADS = 4
A_KV_HEADS = 2
A_HEAD_DIM = 64
A_WINDOW = 128
B_D_INNER = 512
B_HEADDIM = 64
B_HEADS = B_D_INNER // B_HEADDIM
B_GROUPS = 2
B_STATE = 64
B_CONV = 5
C_HEADS = 4
C_QK_DIM = 32
C_V_DIM = 64
N_GROUPS = 4
EXPERTS_PER_GROUP = 8
N_EXPERTS = N_GROUPS * EXPERTS_PER_GROUP
TOP_K = 2

LANES = 128
SUBLANES = 8
TM = 256
CH = 128
MOE_ROWS = 256
NEG = -1e30
VMEM_CAP = 64 * 1024 * 1024

P_AQ = 0
P_AK = 256
P_AV = 512
P_BZ = 768
P_BX = 1280
P_BDT = 2048
P_CQ = 2176
P_CK = 2432
P_CV = 2688
P_COLS = 2944


def _cparams(sem, vmem_bytes):
    limit = int(min(max(2 * vmem_bytes, 16 * 1024 * 1024), VMEM_CAP - 8 * 1024 * 1024))
    return pltpu.CompilerParams(dimension_semantics=sem, vmem_limit_bytes=limit)


def _lane_iota(shape):
    return lax.broadcasted_iota(jnp.int32, shape, len(shape) - 1)


def _row_iota(shape):
    return lax.broadcasted_iota(jnp.int32, shape, len(shape) - 2)


def _silu(v):
    return v * (1.0 / (1.0 + jnp.exp(-v)))


def _dot(a, b):
    return jnp.dot(a, b, preferred_element_type=F32)


def _dot_nt(a, b):
    return lax.dot_general(a, b, (((1,), (1,)), ((), ())), preferred_element_type=F32)


def _mod_kernel(c_ref, w_ref, b_ref, o_ref):
    s = _silu(c_ref[...]).astype(BF16)
    o_ref[0] = _dot(s, w_ref[0].astype(BF16)) + b_ref[0]


def _mod_table(c_all, w_mod, b_mod):
    depth, d, d6 = w_mod.shape
    tn = 1536
    return pl.pallas_call(
        _mod_kernel,
        grid=(depth, d6 // tn),
        in_specs=[
            pl.BlockSpec((SUBLANES, d), lambda l, n: (0, 0)),
            pl.BlockSpec((1, d, tn), lambda l, n: (l, 0, n)),
            pl.BlockSpec((1, 1, tn), lambda l, n: (l, 0, n)),
        ],
        out_specs=pl.BlockSpec((1, SUBLANES, tn), lambda l, n: (l, 0, n)),
        out_shape=jax.ShapeDtypeStruct((depth, SUBLANES, d6), F32),
        compiler_params=_cparams(("parallel", "parallel"), 2 * d * tn * 4 + d * tn * 2),
        name="mod_table",
    )(c_all, w_mod, b_mod.reshape(depth, 1, d6))


def _rope_tables(L, Lc, dim):
    half = dim // 2
    nf = half // 2
    rows = L // GRID_W
    row_pos = jnp.repeat(jnp.arange(rows, dtype=jnp.int32), GRID_W).astype(F32)
    col_pos = jnp.tile(jnp.arange(GRID_W, dtype=jnp.int32), rows).astype(F32)
    freqs = ROPE_BASE ** (-jnp.arange(nf, dtype=F32) / nf)
    d = jnp.arange(dim)
    within = d % half
    f = freqs[within % nf]
    first = within < nf
    pos = jnp.where((d // half == 0)[None, :], row_pos[:, None], col_pos[:, None])
    ang = pos * f[None, :]
    cos = jnp.cos(ang)
    sin = jnp.where(first[None, :], -jnp.sin(ang), jnp.sin(ang))
    cos = jnp.concatenate([jnp.ones((Lc, dim), F32), cos], axis=0)
    sin = jnp.concatenate([jnp.zeros((Lc, dim), F32), sin], axis=0)
    reps = LANES // dim
    return jnp.tile(cos, (1, reps)), jnp.tile(sin, (1, reps))


def _rope(t, cos, sin, nf):
    w = t.shape[1]
    reps = w // LANES
    cosw = jnp.concatenate([cos] * reps, axis=1) if reps > 1 else cos
    sinw = jnp.concatenate([sin] * reps, axis=1) if reps > 1 else sin
    lane = _lane_iota(t.shape)
    first = (lane & (2 * nf - 1)) < nf
    partner = jnp.where(first, pltpu.roll(t, w - nf, axis=1), pltpu.roll(t, nf, axis=1))
    return t * cosw + partner * sinw


def _inproj_kernel(*refs, has_moe):
    if has_moe:
        (x_ref, ya_ref, yb_ref, rt_ref, modp_ref, mod_ref, n1_ref, w_ref, cosa_ref, sina_ref, cosc_ref,
         sinc_ref, xo_ref, qa_ref, ka_ref, va_ref, z_ref, xb_ref, qc_ref, kc_ref, vc_ref) = refs
        rt = rt_ref[0]
        y = rt[:, 2:3] * ya_ref[0] + rt[:, 3:4] * yb_ref[0]
        x = x_ref[0] + modp_ref[0, 5:6, :] * y
        xo_ref[0] = x
    else:
        (x_ref, mod_ref, n1_ref, w_ref, cosa_ref, sina_ref, cosc_ref, sinc_ref,
         qa_ref, ka_ref, va_ref, z_ref, xb_ref, qc_ref, kc_ref, vc_ref) = refs
        x = x_ref[0]
    ms = jnp.mean(x * x, axis=-1, keepdims=True)
    h = x * lax.rsqrt(ms + EPS) * n1_ref[...]
    h = h * (1.0 + mod_ref[0, 1:2, :]) + mod_ref[0, 0:1, :]
    hb = h.astype(BF16)

    def proj(lo, hi):
        return _dot(hb, w_ref[:, lo:hi])

    cosa, sina = cosa_ref[...], sina_ref[...]
    cosc, sinc = cosc_ref[...], sinc_ref[...]
    nfa = A_HEAD_DIM // 4
    nfc = C_QK_DIM // 4
    qa_ref[0] = (_rope(proj(P_AQ, P_AK), cosa, sina, nfa) * (A_HEAD_DIM ** -0.5)).astype(BF16)
    ka_ref[0] = _rope(proj(P_AK, P_AV), cosa, sina, nfa).astype(BF16)
    va_ref[0] = proj(P_AV, P_BZ).astype(BF16)
    z_ref[0] = proj(P_BZ, P_BX)
    xb_ref[0] = proj(P_BX, P_CQ)
    qc_ref[0] = (_rope(proj(P_CQ, P_CK), cosc, sinc, nfc) * (C_QK_DIM ** -0.5)).astype(BF16)
    kc_ref[0] = _rope(proj(P_CK, P_CV), cosc, sinc, nfc).astype(BF16)
    vc_ref[0] = proj(P_CV, P_COLS).astype(BF16)


def _mod_index(nb):
    return lambda b, j: (jnp.where(j == 0, nb, b), 0, 0)


def _inproj(x, moe, mod_prev, mod_l, n1w, w_p, tabs):
    B, S, D = x.shape
    nblk = S // TM
    row = lambda w: pl.BlockSpec((1, TM, w), lambda b, j: (b, j, 0))
    tab = pl.BlockSpec((TM, LANES), lambda b, j: (j, 0))
    modspec = pl.BlockSpec((1, 6, D), _mod_index(B))
    in_specs, args = [row(D)], [x]
    has_moe = moe is not None
    if has_moe:
        ya, yb, route = moe
        in_specs += [row(D), row(D), row(LANES), modspec]
        args += [ya, yb, route, mod_prev]
    in_specs += [modspec, pl.BlockSpec((1, D), lambda b, j: (0, 0)),
                 pl.BlockSpec((D, P_COLS), lambda b, j: (0, 0)), tab, tab, tab, tab]
    args += [mod_l, n1w.reshape(1, D), w_p, *tabs]
    widths = [(256, BF16), (256, BF16), (256, BF16), (512, F32), (896, F32), (256, BF16), (256, BF16),
              (256, BF16)]
    out_specs = [row(w) for w, _ in widths]
    out_shape = [jax.ShapeDtypeStruct((B, S, w), dt) for w, dt in widths]
    if has_moe:
        out_specs = [row(D)] + out_specs
        out_shape = [jax.ShapeDtypeStruct((B, S, D), F32)] + out_shape
    vmem = 2 * D * P_COLS * 2 + 8 * TM * D * 4 + 4 * TM * P_COLS * 4
    return pl.pallas_call(
        functools.partial(_inproj_kernel, has_moe=has_moe),
        grid=(B, nblk), in_specs=in_specs, out_specs=out_specs, out_shape=out_shape,
        compiler_params=_cparams(("parallel", "parallel"), vmem),
        name="inproj",
    )(*args)


def _attn_a_kernel(sink_ref, q_ref, k_ref, v_ref, o_ref, *, lc):
    j = pl.program_id(1)
    S = k_ref.shape[1]
    span = TM + 2 * A_WINDOW
    start = jnp.clip(j * TM - A_WINDOW, 0, S - span)
    start = pl.multiple_of(start, LANES)
    qrow = j * TM + _row_iota((TM, span))
    krow = start + _lane_iota((TM, span))
    mask = (jnp.abs(krow - qrow) <= A_WINDOW) & (krow >= lc) & (j > 0)
    mask2 = jnp.concatenate([mask, mask], axis=0)
    lane = _lane_iota((TM, LANES))
    low = lane < A_HEAD_DIM
    top = _row_iota((2 * TM, 1)) < TM
    for c in range(A_KV_HEADS):
        cs = slice(c * LANES, (c + 1) * LANES)
        q = q_ref[0, :, cs]
        zero = jnp.zeros_like(q)
        qs = jnp.concatenate([jnp.where(low, q, zero), jnp.where(low, zero, q)], axis=0)
        kl = k_ref[0, pl.ds(start, span), cs]
        vl = v_ref[0, pl.ds(start, span), cs]
        kc = k_ref[0, 0:lc, cs]
        vc = v_ref[0, 0:lc, cs]
        s_l = jnp.where(mask2, _dot_nt(qs, kl), NEG)
        s_c = _dot_nt(qs, kc)
        sink = jnp.where(top, sink_ref[2 * c], sink_ref[2 * c + 1])
        m = jnp.maximum(jnp.maximum(jnp.max(s_l, axis=-1, keepdims=True),
                                    jnp.max(s_c, axis=-1, keepdims=True)), sink)
        p_l = jnp.exp(s_l - m)
        p_c = jnp.exp(s_c - m)
        den = (jnp.sum(p_l, axis=-1, keepdims=True) + jnp.sum(p_c, axis=-1, keepdims=True)
               + jnp.exp(sink - m))
        o = (_dot(p_l.astype(BF16), vl) + _dot(p_c.astype(BF16), vc)) * (1.0 / den)
        o_ref[0, :, cs] = jnp.where(low, o[:TM], o[TM:]).astype(BF16)


def _attn_a(sink, qa, ka, va, lc):
    B, S, W = qa.shape
    nblk = S // TM
    blk = pl.BlockSpec((1, TM, W), lambda b, j: (b, j, 0))
    full = pl.BlockSpec((1, S, W), lambda b, j: (b, 0, 0))
    return pl.pallas_call(
        functools.partial(_attn_a_kernel, lc=lc),
        grid=(B, nblk),
        in_specs=[pl.BlockSpec(memory_space=pltpu.SMEM), blk, full, full],
        out_specs=blk,
        out_shape=jax.ShapeDtypeStruct((B, S, W), BF16),
        compiler_params=_cparams(("parallel", "parallel"), 4 * S * W * 2 + 12 * 2 * TM * 768 * 4),
        name="attn_a",
    )(sink, qa, ka, va)


def _attn_c_body(q_ref, k_ref, v_ref, cl_ref, sw_ref, o_ref, nk, lambda_init):
    cl = cl_ref[...]
    lam = (jnp.exp(jnp.sum(cl[0:1] * cl[1:2], axis=-1, keepdims=True))
           - jnp.exp(jnp.sum(cl[2:3] * cl[3:4], axis=-1, keepdims=True)) + lambda_init)
    lane = _lane_iota((CH, LANES))
    seg_r = _row_iota((LANES, LANES)) // C_V_DIM
    seg_c = _lane_iota((LANES, LANES)) // C_V_DIM
    headsum = (seg_r == seg_c).astype(F32)
    for hp in range(C_HEADS // 2):
        cs = slice(hp * LANES, (hp + 1) * LANES)
        q = q_ref[0, :, cs]
        zero = jnp.zeros_like(q)
        k = k_ref[0, 0:nk, cs]
        v = v_ref[0, 0:nk, cs]
        outs = []
        for hh in range(2):
            base = hh * 2 * C_QK_DIM
            in0 = (lane >= base) & (lane < base + C_QK_DIM)
            in1 = (lane >= base + C_QK_DIM) & (lane < base + 2 * C_QK_DIM)
            qs = jnp.concatenate([jnp.where(in0, q, zero), jnp.where(in1, q, zero)], axis=0)
            s = _dot_nt(qs, k)
            m = jnp.max(s, axis=-1, keepdims=True)
            e = jnp.exp(s - m)
            inv = 1.0 / jnp.sum(e, axis=-1, keepdims=True)
            w = e[:CH] * inv[:CH] - e[CH:] * (lam * inv[CH:])
            outs.append(_dot(w.astype(BF16), v))
        o = jnp.where(lane < C_V_DIM, outs[0], outs[1])
        ss = jnp.dot(o * o, headsum, precision=HI, preferred_element_type=F32)
        y = o * lax.rsqrt(ss * (1.0 / C_V_DIM) + EPS) * sw_ref[...] * (1.0 - lambda_init)
        o_ref[0, :, cs] = y.astype(BF16)


def _attn_c_kernel(q_ref, k_ref, v_ref, cl_ref, sw_ref, o_ref, *, lc, lambda_init):
    j = pl.program_id(1)
    S = k_ref.shape[1]

    @pl.when(j < lc // CH)
    def _():
        _attn_c_body(q_ref, k_ref, v_ref, cl_ref, sw_ref, o_ref, lc, lambda_init)

    @pl.when(j >= lc // CH)
    def _():
        _attn_c_body(q_ref, k_ref, v_ref, cl_ref, sw_ref, o_ref, S, lambda_init)


def _attn_c(qc, kc, vc, c_lambda, subln_w, lc, lambda_init):
    B, S, W = qc.shape
    blk = pl.BlockSpec((1, CH, W), lambda b, j: (b, j, 0))
    full = pl.BlockSpec((1, S, W), lambda b, j: (b, 0, 0))
    sw = jnp.tile(subln_w, LANES // C_V_DIM).reshape(1, LANES)
    return pl.pallas_call(
        functools.partial(_attn_c_kernel, lc=lc, lambda_init=lambda_init),
        grid=(B, S // CH),
        in_specs=[blk, full, full, pl.BlockSpec((4, C_QK_DIM), lambda b, j: (0, 0)),
                  pl.BlockSpec((1, LANES), lambda b, j: (0, 0))],
        out_specs=blk,
        out_shape=jax.ShapeDtypeStruct((B, S, W), BF16),
        compiler_params=_cparams(("parallel", "parallel"), 4 * S * W * 2 + 5 * 2 * CH * S * 4),
        name="attn_c",
    )(qc, kc, vc, c_lambda, sw)


XBC_W = B_D_INNER + 2 * B_GROUPS * B_STATE
XB_W = XBC_W + LANES


def _ssd_prep(prev_ref, cur_ref, next_ref, cw_ref, cb_ref, dtb_ref, alog_ref, c, ncc, nch):
    cur = cur_ref[0]
    first = (c == 0) | (c == ncc)
    last = (c == ncc - 1) | (c == nch - 1)
    prev = jnp.where(first, 0.0, prev_ref[0][:, :XBC_W])
    nxt = jnp.where(last, 0.0, next_ref[0][:, :XBC_W])
    ext = jnp.concatenate([prev, cur[:, :XBC_W], nxt], axis=0)
    rows = CH + 2 * SUBLANES
    acc = jnp.zeros((CH, XBC_W), F32) + cb_ref[...]
    for k in range(B_CONV):
        sh = (B_CONV // 2 - k) % rows
        r = ext if sh == 0 else pltpu.roll(ext, sh, axis=0)
        acc = acc + r[SUBLANES:SUBLANES + CH] * cw_ref[k:k + 1, :]
    u = _silu(acc)
    xs = u[:, :B_D_INNER]
    bm = u[:, B_D_INNER:B_D_INNER + LANES]
    cm = u[:, B_D_INNER + LANES:]
    xdt_raw = cur[:, XBC_W:] + dtb_ref[...]
    dt = jnp.maximum(xdt_raw, 0.0) + jnp.log(1.0 + jnp.exp(-jnp.abs(xdt_raw)))
    dta = dt * (-jnp.exp(alog_ref[...]))
    li = _row_iota((CH, CH))
    si = _lane_iota((CH, CH))
    tl = (si <= li).astype(F32)
    tu = (si >= li).astype(F32)
    lane = _lane_iota((CH, LANES))
    cum = jnp.where(lane < B_HEADS,
                    jnp.dot(tl, dta, precision=HI, preferred_element_type=F32),
                    jnp.dot(tu, dta, precision=HI, preferred_element_type=F32))
    er = _row_iota((LANES, B_D_INNER))
    ec = _lane_iota((LANES, B_D_INNER)) // B_HEADDIM
    dtx, cumx = [], []
    for d in range(2):
        e = (er == ec + d * B_HEADS).astype(F32)
        dtx.append(jnp.dot(dt, e, precision=HI, preferred_element_type=F32))
        cumx.append(jnp.dot(cum, e, precision=HI, preferred_element_type=F32))
    aend = [cumx[0][CH - 1:CH, :], cumx[1][0:1, :]]
    return xs, bm, cm, cum, dtx, cumx, aend


def _ssd_specs(S, B):
    n8 = S // SUBLANES
    per = CH // SUBLANES
    prev = pl.BlockSpec((1, SUBLANES, XB_W), lambda b, c: (b, jnp.maximum(c * per - 1, 0), 0))
    cur = pl.BlockSpec((1, CH, XB_W), lambda b, c: (b, c, 0))
    nxt = pl.BlockSpec((1, SUBLANES, XB_W), lambda b, c: (b, jnp.minimum(c * per + per, n8 - 1), 0))
    const = lambda r, w: pl.BlockSpec((r, w), lambda b, c: (0, 0))
    return [prev, cur, nxt, const(B_CONV, XBC_W), const(1, XBC_W), const(1, LANES), const(1, LANES)]


def _ssd_state_kernel(prev_ref, cur_ref, next_ref, cw_ref, cb_ref, dtb_ref, alog_ref, s_ref, da_ref,
                      *, ncc, nch):
    c = pl.program_id(1)
    xs, bm, cm, cum, dtx, cumx, aend = _ssd_prep(prev_ref, cur_ref, next_ref, cw_ref, cb_ref, dtb_ref,
                                                 alog_ref, c, ncc, nch)
    bmt = bm.T.astype(BF16)
    lane = _lane_iota((B_STATE, B_D_INNER))
    for d in range(2):
        xdec = (xs * dtx[d] * jnp.exp(aend[d] - cumx[d])).astype(BF16)
        s2 = _dot(bmt, xdec)
        s_ref[0, 0, d] = jnp.where(lane < B_D_INNER // 2, s2[:B_STATE], s2[B_STATE:])
        da_ref[0, 0, d] = jnp.broadcast_to(jnp.exp(aend[d]), (SUBLANES, B_D_INNER))


def _ssd_scan_kernel(s_ref, da_ref, h_ref, *, ncc, nch):
    order_f = list(range(nch))
    order_b = list(range(ncc - 1, -1, -1)) + list(range(nch - 1, ncc - 1, -1))
    for d, order in enumerate((order_f, order_b)):
        h = jnp.zeros((B_STATE, B_D_INNER), F32)
        for c in order:
            h_ref[0, c, d] = h
            h = da_ref[0, c, d, 0:1, :] * h + s_ref[0, c, d]


def _ssd_out_kernel(prev_ref, cur_ref, next_ref, cw_ref, cb_ref, dtb_ref, alog_ref, z_ref, h_ref, dsk_ref,
                    nw_ref, o_ref, *, ncc, nch):
    c = pl.program_id(1)
    xs, bm, cm, cum, dtx, cumx, aend = _ssd_prep(prev_ref, cur_ref, next_ref, cw_ref, cb_ref, dtb_ref,
                                                 alog_ref, c, ncc, nch)
    half = B_D_INNER // 2
    lane_s = _lane_iota((CH, LANES))
    lane_h = _lane_iota((B_STATE, B_D_INNER))
    lane_y = _lane_iota((CH, half)) // B_HEADDIM
    cmb = cm.astype(BF16)
    bmb = bm.astype(BF16)
    zero = jnp.zeros_like(cmb)
    gmat = [_dot_nt(jnp.where((lane_s // B_STATE) == g, cmb, zero), bmb) for g in range(B_GROUPS)]
    cumt = cum.T
    li = _row_iota((CH, CH))
    si = _lane_iota((CH, CH))
    y = xs * (dsk_ref[0:1, :] + dsk_ref[1:2, :])
    for d in range(2):
        tri = (si <= li) if d == 0 else (si >= li)
        hc = h_ref[0, 0, d]
        h2 = jnp.concatenate([jnp.where(lane_h < half, hc, 0.0), jnp.where(lane_h < half, 0.0, hc)], axis=0)
        y = y + _dot(cmb, h2.astype(BF16)) * jnp.exp(cumx[d])
        xdt = (xs * dtx[d]).astype(BF16)
        parts = []
        for g in range(B_GROUPS):
            acc = jnp.zeros((CH, half), F32)
            for hl in range(B_HEADS // B_GROUPS):
                idx = d * B_HEADS + g * (B_HEADS // B_GROUPS) + hl
                seg = cum[:, idx:idx + 1] - cumt[idx:idx + 1, :]
                dec = jnp.where(tri, jnp.exp(jnp.where(tri, seg, 0.0)), 0.0)
                sc = (gmat[g] * dec).astype(BF16)
                yh = _dot(sc, xdt[:, g * half:(g + 1) * half])
                acc = jnp.where(lane_y == hl, yh, acc)
            parts.append(acc)
        y = y + jnp.concatenate(parts, axis=1)
    gz = y * _silu(z_ref[0])
    outs = []
    for g in range(B_GROUPS):
        gg = gz[:, g * half:(g + 1) * half]
        ms = jnp.mean(gg * gg, axis=-1, keepdims=True)
        outs.append(gg * lax.rsqrt(ms + EPS))
    o_ref[0] = (jnp.concatenate(outs, axis=1) * nw_ref[...]).astype(BF16)


def _ssd(z, xb, conv_w, conv_b, dt_bias, a_log, d_skip, norm_w, lc):
    B, S, _ = z.shape
    nch, ncc = S // CH, lc // CH
    pad = lambda v: jnp.pad(v.reshape(1, -1), ((0, 0), (0, LANES - v.size)))
    consts = [conv_w, conv_b.reshape(1, XBC_W), pad(dt_bias), pad(a_log)]
    specs = _ssd_specs(S, B)
    st_spec = pl.BlockSpec((1, 1, 2, B_STATE, B_D_INNER), lambda b, c: (b, c, 0, 0, 0))
    da_spec = pl.BlockSpec((1, 1, 2, SUBLANES, B_D_INNER), lambda b, c: (b, c, 0, 0, 0))
    st_shape = jax.ShapeDtypeStruct((B, nch, 2, B_STATE, B_D_INNER), F32)
    da_shape = jax.ShapeDtypeStruct((B, nch, 2, SUBLANES, B_D_INNER), F32)
    small = 16 * CH * XB_W * 4
    states, da = pl.pallas_call(
        functools.partial(_ssd_state_kernel, ncc=ncc, nch=nch),
        grid=(B, nch), in_specs=specs, out_specs=[st_spec, da_spec], out_shape=[st_shape, da_shape],
        compiler_params=_cparams(("parallel", "parallel"), small),
        name="ssd_state",
    )(xb, xb, xb, *consts)
    st_bytes = nch * 2 * B_STATE * B_D_INNER * 4
    hstart = pl.pallas_call(
        functools.partial(_ssd_scan_kernel, ncc=ncc, nch=nch),
        grid=(B,),
        in_specs=[pl.BlockSpec((1, nch, 2, B_STATE, B_D_INNER), lambda b: (b, 0, 0, 0, 0)),
                  pl.BlockSpec((1, nch, 2, SUBLANES, B_D_INNER), lambda b: (b, 0, 0, 0, 0))],
        out_specs=pl.BlockSpec((1, nch, 2, B_STATE, B_D_INNER), lambda b: (b, 0, 0, 0, 0)),
        out_shape=st_shape,
        compiler_params=_cparams(("parallel",), 5 * st_bytes // 2),
        name="ssd_scan",
    )(states, da)
    dsk = jnp.repeat(d_skip, B_HEADDIM, axis=1)
    return pl.pallas_call(
        functools.partial(_ssd_out_kernel, ncc=ncc, nch=nch),
        grid=(B, nch),
        in_specs=specs + [pl.BlockSpec((1, CH, B_D_INNER), lambda b, c: (b, c, 0)), st_spec,
                          pl.BlockSpec((2, B_D_INNER), lambda b, c: (0, 0)),
                          pl.BlockSpec((1, B_D_INNER), lambda b, c: (0, 0))],
        out_specs=pl.BlockSpec((1, CH, B_D_INNER), lambda b, c: (b, c, 0)),
        out_shape=jax.ShapeDtypeStruct((B, S, B_D_INNER), BF16),
        compiler_params=_cparams(("parallel", "parallel"), small),
        name="ssd_out",
    )(xb, xb, xb, *consts, z, hstart, dsk, norm_w.reshape(1, B_D_INNER))


def _outproj_kernel(oa_ref, ob_ref, oc_ref, x_ref, mod_ref, n2_ref, w_ref, wr_ref, xo_ref, h2_ref, rt_ref):
    a_w = A_HEADS * A_HEAD_DIM
    mix = (_dot(oa_ref[0], w_ref[0:a_w, :]) + _dot(ob_ref[0], w_ref[a_w:a_w + B_D_INNER, :])
           + _dot(oc_ref[0], w_ref[a_w + B_D_INNER:, :]))
    x = x_ref[0] + mod_ref[0, 2:3, :] * mix
    xo_ref[0] = x
    ms = jnp.mean(x * x, axis=-1, keepdims=True)
    h2 = x * lax.rsqrt(ms + EPS) * n2_ref[...]
    h2 = h2 * (1.0 + mod_ref[0, 4:5, :]) + mod_ref[0, 3:4, :]
    h2_ref[0] = h2.astype(BF16)
    logit = jnp.dot(h2, wr_ref[...], precision=HI, preferred_element_type=F32)
    lf = _lane_iota(logit.shape).astype(F32)
    big = 1e9
    gmask = lf < N_GROUPS
    gl = jnp.where(gmask, logit, NEG)
    gm = jnp.max(gl, axis=-1, keepdims=True)
    g_p = 1.0 / jnp.sum(jnp.exp(gl - gm), axis=-1, keepdims=True)
    g_sel = jnp.min(jnp.where(gmask & (gl == gm), lf, big), axis=-1, keepdims=True)
    lo = N_GROUPS + EXPERTS_PER_GROUP * g_sel
    emask = (lf >= lo) & (lf < lo + EXPERTS_PER_GROUP)
    el = jnp.where(emask, logit, NEG)
    v1 = jnp.max(el, axis=-1, keepdims=True)
    i1 = jnp.min(jnp.where(emask & (el == v1), lf, big), axis=-1, keepdims=True)
    rest = emask & (lf != i1)
    el2 = jnp.where(rest, logit, NEG)
    v2 = jnp.max(el2, axis=-1, keepdims=True)
    i2 = jnp.min(jnp.where(rest & (el2 == v2), lf, big), axis=-1, keepdims=True)
    t = jnp.exp(v2 - v1)
    gate1 = g_p / (1.0 + t)
    gate2 = g_p * t / (1.0 + t)
    info = jnp.where(lf == 0, i1 - N_GROUPS,
                     jnp.where(lf == 1, i2 - N_GROUPS, jnp.where(lf == 2, gate1, jnp.where(lf == 3, gate2, 0.0))))
    rt_ref[0] = info


def _outproj(oa, ob, oc, x, mod_l, n2w, w_out, w_r):
    B, S, D = x.shape
    row = lambda w: pl.BlockSpec((1, TM, w), lambda b, j: (b, j, 0))
    mixw = w_out.shape[0]
    return pl.pallas_call(
        _outproj_kernel,
        grid=(B, S // TM),
        in_specs=[row(oa.shape[2]), row(ob.shape[2]), row(oc.shape[2]), row(D),
                  pl.BlockSpec((1, 6, D), _mod_index(B)), pl.BlockSpec((1, D), lambda b, j: (0, 0)),
                  pl.BlockSpec((mixw, D), lambda b, j: (0, 0)), pl.BlockSpec((D, LANES), lambda b, j: (0, 0))],
        out_specs=[row(D), row(D), row(LANES)],
        out_shape=[jax.ShapeDtypeStruct((B, S, D), F32), jax.ShapeDtypeStruct((B, S, D), BF16),
                   jax.ShapeDtypeStruct((B, S, LANES), F32)],
        compiler_params=_cparams(("parallel", "parallel"), 2 * mixw * D * 2 + 12 * TM * D * 4),
        name="outproj",
    )(oa, ob, oc, x, mod_l, n2w.reshape(1, D), w_out, w_r)


def _expert_kernel(be_ref, nu_ref, x_ref, wg_ref, wu_ref, wd_ref, o_ref):
    i = pl.program_id(0)

    @pl.when(i < nu_ref[0])
    def _():
        xb = x_ref[...]
        hid = _silu(_dot(xb, wg_ref[0].astype(BF16))) * _dot(xb, wu_ref[0].astype(BF16))
        o_ref[...] = _dot(hid.astype(BF16), wd_ref[0].astype(BF16))

    @pl.when(i >= nu_ref[0])
    def _():
        o_ref[...] = jnp.zeros_like(o_ref)


def _expert_mlp(buf, block_e, n_used, w_gate, w_up, w_down):
    rows, D = buf.shape
    de = w_gate.shape[2]
    nblk = rows // MOE_ROWS
    grid_spec = pltpu.PrefetchScalarGridSpec(
        num_scalar_prefetch=2, grid=(nblk,),
        in_specs=[pl.BlockSpec((MOE_ROWS, D), lambda i, be, nu: (i, 0)),
                  pl.BlockSpec((1, D, de), lambda i, be, nu: (be[i], 0, 0)),
                  pl.BlockSpec((1, D, de), lambda i, be, nu: (be[i], 0, 0)),
                  pl.BlockSpec((1, de, D), lambda i, be, nu: (be[i], 0, 0))],
        out_specs=pl.BlockSpec((MOE_ROWS, D), lambda i, be, nu: (i, 0)))
    return pl.pallas_call(
        _expert_kernel, grid_spec=grid_spec,
        out_shape=jax.ShapeDtypeStruct((rows, D), F32),
        compiler_params=_cparams(("arbitrary",), 2 * 3 * D * de * 4 + 3 * D * de * 2 + 8 * MOE_ROWS * D * 4),
        name="expert_mlp",
    )(block_e, n_used, buf, w_gate, w_up, w_down)


def _dispatch_plan(e_idx):
    T = e_idx.shape[0]
    A = T * TOP_K
    flat_e = e_idx.reshape(-1)
    order = jnp.argsort(flat_e)
    sorted_e = flat_e[order]
    counts = jnp.bincount(flat_e, length=N_EXPERTS)
    padded = (counts + MOE_ROWS - 1) // MOE_ROWS * MOE_ROWS
    pad_end = jnp.cumsum(padded)
    pad_start = pad_end - padded
    start = jnp.cumsum(counts) - counts
    dest = (pad_start[sorted_e] + jnp.arange(A) - start[sorted_e]).astype(jnp.int32)
    n_blocks = -(-A // MOE_ROWS) + N_EXPERTS
    src_tok = jnp.zeros((n_blocks * MOE_ROWS,), jnp.int32).at[dest].set((order // TOP_K).astype(jnp.int32))
    pos = jnp.zeros((A,), jnp.int32).at[order].set(dest).reshape(T, TOP_K)
    block_e = jnp.minimum(jnp.searchsorted(pad_end, jnp.arange(n_blocks) * MOE_ROWS, side='right'),
                          N_EXPERTS - 1).astype(jnp.int32)
    n_used = (pad_end[-1] // MOE_ROWS).astype(jnp.int32).reshape(1)
    return src_tok, pos, block_e, n_used


def _final_kernel(x_ref, ya_ref, yb_ref, rt_ref, modp_ref, nw_ref, o_ref):
    rt = rt_ref[0]
    y = rt[:, 2:3] * ya_ref[0] + rt[:, 3:4] * yb_ref[0]
    x = x_ref[0] + modp_ref[0, 5:6, :] * y
    ms = jnp.mean(x * x, axis=-1, keepdims=True)
    o_ref[0] = x * lax.rsqrt(ms + EPS) * nw_ref[...]


def _final(x, ya, yb, route, mod_prev, nw, lc):
    B, S, D = x.shape
    off = lc // TM
    row = lambda w: pl.BlockSpec((1, TM, w), lambda b, j: (b, j + off, 0))
    return pl.pallas_call(
        _final_kernel,
        grid=(B, (S - lc) // TM),
        in_specs=[row(D), row(D), row(D), row(LANES), pl.BlockSpec((1, 6, D), lambda b, j: (b, 0, 0)),
                  pl.BlockSpec((1, D), lambda b, j: (0, 0))],
        out_specs=pl.BlockSpec((1, TM, D), lambda b, j: (b, j, 0)),
        out_shape=jax.ShapeDtypeStruct((B, S - lc, D), F32),
        compiler_params=_cparams(("parallel", "parallel"), 10 * TM * D * 4),
        name="final_norm",
    )(x, ya, yb, route, mod_prev, nw.reshape(1, D))


def _pack_w_in(w):
    d = w.shape[0]
    aq, ak, av = w[:, 0:256], w[:, 256:384], w[:, 384:512]
    dup = lambda m: jnp.concatenate([m[:, 0:64], m[:, 0:64], m[:, 64:128], m[:, 64:128]], axis=1)
    bz, bx, bdt = w[:, 512:1024], w[:, 1024:1792], w[:, 1792:1808]
    cq, ck, cv = w[:, 1808:2064], w[:, 2064:2320], w[:, 2320:2576]
    packed = jnp.concatenate([aq, dup(ak), dup(av), bz, bx, bdt, jnp.zeros((d, LANES - 16), w.dtype),
                              cq, ck, cv], axis=1)
    return packed.astype(BF16)


def kernel(x, c, ctx, c_ctx, w_mod, b_mod, norm1_w, norm2_w, w_in, w_out, a_sink, b_conv_w, b_conv_b,
           b_dt_bias, b_a_log, b_d, b_norm_w, c_lambda, c_subln_w, moe_group_router, moe_router,
           moe_w_gate, moe_w_up, moe_w_down, final_norm_w):
    B, L, D = x.shape
    Lc = ctx.shape[1]
    depth = w_mod.shape[0]
    assert Lc == TM and L % TM == 0 and B + 1 <= SUBLANES
    S = Lc + L
    T = B * S
    xa = jnp.concatenate([ctx, x], axis=1)
    c_all = jnp.concatenate([c, c_ctx[None, :], jnp.zeros((SUBLANES - B - 1, D), F32)], axis=0)
    mod = _mod_table(c_all, w_mod, b_mod)
    tabs = _rope_tables(L, Lc, A_HEAD_DIM) + _rope_tables(L, Lc, C_QK_DIM)
    moe, mod_prev = None, None
    for l in range(depth):
        mod_l = mod[l].reshape(SUBLANES, 6, D)
        lambda_init = 0.8 - 0.6 * math.exp(-0.3 * l)
        outs = _inproj(xa, moe, mod_prev, mod_l, norm1_w[l], _pack_w_in(w_in[l]), tabs)
        if moe is not None:
            xa, outs = outs[0], outs[1:]
        qa, ka, va, z, xb, qc, kc, vc = outs
        oa = _attn_a(a_sink[l], qa, ka, va, Lc)
        ob = _ssd(z, xb, b_conv_w[l], b_conv_b[l], b_dt_bias[l], b_a_log[l], b_d[l], b_norm_w[l], Lc)
        oc = _attn_c(qc, kc, vc, c_lambda[l], c_subln_w[l], Lc, lambda_init)
        w_r = jnp.concatenate([moe_group_router[l], moe_router[l],
                               jnp.zeros((D, LANES - N_GROUPS - N_EXPERTS), F32)], axis=1)
        xa, h2, route = _outproj(oa, ob, oc, xa, mod_l, norm2_w[l], w_out[l].astype(BF16), w_r)
        e_idx = route[..., 0:TOP_K].astype(jnp.int32).reshape(T, TOP_K)
        src_tok, pos, block_e, n_used = _dispatch_plan(e_idx)
        buf = jnp.take(h2.reshape(T, D), src_tok, axis=0)
        eo = _expert_mlp(buf, block_e, n_used, moe_w_gate[l], moe_w_up[l], moe_w_down[l])
        ya = jnp.take(eo, pos[:, 0], axis=0).reshape(B, S, D)
        yb = jnp.take(eo, pos[:, 1], axis=0).reshape(B, S, D)
        moe, mod_prev = (ya, yb, route), mod_l
    return _final(xa, moe[0], moe[1], moe[2], mod_prev, final_norm_w, Lc)
```

```python
import functools
import math

import jax
import jax.numpy as jnp
from jax import lax
from jax.experimental import pallas as pl
from jax.experimental.pallas import tpu as pltpu

F32 = jnp.float32
BF16 = jnp.bfloat16
HI = lax.Precision.HIGHEST

GRID_W = 64
EPS = 1e-6
ROPE_BASE = 10000.0

A_HEADS = 4
A_KV_HEADS = 2
A_HEAD_DIM = 64
A_WINDOW = 128
B_D_INNER = 512
B_HEADDIM = 64
B_HEADS = B_D_INNER // B_HEADDIM
B_GROUPS = 2
B_STATE = 64
B_CONV = 5
C_HEADS = 4
C_QK_DIM = 32
C_V_DIM = 64
N_GROUPS = 4
EXPERTS_PER_GROUP = 8
N_EXPERTS = N_GROUPS * EXPERTS_PER_GROUP
TOP_K = 2

LANES = 128
SUBLANES = 8
TM = 256
CH = 128
MOE_ROWS = 256
NEG = -1e30
VMEM_CAP = 64 * 1024 * 1024

P_AQ = 0
P_AK = 256
P_AV = 512
P_BZ = 768
P_BX = 1280
P_BDT = 2048
P_CQ = 2176
P_CK = 2432
P_CV = 2688
P_COLS = 2944


def _cparams(sem, vmem_bytes):
    limit = int(min(max(2 * vmem_bytes, 16 * 1024 * 1024), VMEM_CAP - 8 * 1024 * 1024))
    return pltpu.CompilerParams(dimension_semantics=sem, vmem_limit_bytes=limit)


def _lane_iota(shape):
    return lax.broadcasted_iota(jnp.int32, shape, len(shape) - 1)


def _row_iota(shape):
    return lax.broadcasted_iota(jnp.int32, shape, len(shape) - 2)


def _silu(v):
    return v * (1.0 / (1.0 + jnp.exp(-v)))


def _dot(a, b):
    return jnp.dot(a, b, preferred_element_type=F32)


def _dot_nt(a, b):
    return lax.dot_general(a, b, (((1,), (1,)), ((), ())), preferred_element_type=F32)


def _mod_kernel(c_ref, w_ref, b_ref, o_ref):
    s = _silu(c_ref[...]).astype(BF16)
    o_ref[0] = _dot(s, w_ref[0].astype(BF16)) + b_ref[0]


def _mod_table(c_all, w_mod, b_mod):
    depth, d, d6 = w_mod.shape
    tn = 1536
    return pl.pallas_call(
        _mod_kernel,
        grid=(depth, d6 // tn),
        in_specs=[
            pl.BlockSpec((SUBLANES, d), lambda l, n: (0, 0)),
            pl.BlockSpec((1, d, tn), lambda l, n: (l, 0, n)),
            pl.BlockSpec((1, 1, tn), lambda l, n: (l, 0, n)),
        ],
        out_specs=pl.BlockSpec((1, SUBLANES, tn), lambda l, n: (l, 0, n)),
        out_shape=jax.ShapeDtypeStruct((depth, SUBLANES, d6), F32),
        compiler_params=_cparams(("parallel", "parallel"), 2 * d * tn * 4 + d * tn * 2),
        name="mod_table",
    )(c_all, w_mod, b_mod.reshape(depth, 1, d6))


def _rope_tables(L, Lc, dim):
    half = dim // 2
    nf = half // 2
    rows = L // GRID_W
    row_pos = jnp.repeat(jnp.arange(rows, dtype=jnp.int32), GRID_W).astype(F32)
    col_pos = jnp.tile(jnp.arange(GRID_W, dtype=jnp.int32), rows).astype(F32)
    freqs = ROPE_BASE ** (-jnp.arange(nf, dtype=F32) / nf)
    d = jnp.arange(dim)
    within = d % half
    f = freqs[within % nf]
    first = within < nf
    pos = jnp.where((d // half == 0)[None, :], row_pos[:, None], col_pos[:, None])
    ang = pos * f[None, :]
    cos = jnp.cos(ang)
    sin = jnp.where(first[None, :], -jnp.sin(ang), jnp.sin(ang))
    cos = jnp.concatenate([jnp.ones((Lc, dim), F32), cos], axis=0)
    sin = jnp.concatenate([jnp.zeros((Lc, dim), F32), sin], axis=0)
    reps = LANES // dim
    return jnp.tile(cos, (1, reps)), jnp.tile(sin, (1, reps))


def _rope(t, cos, sin, nf):
    w = t.shape[1]
    reps = w // LANES
    cosw = jnp.concatenate([cos] * reps, axis=1) if reps > 1 else cos
    sinw = jnp.concatenate([sin] * reps, axis=1) if reps > 1 else sin
    lane = _lane_iota(t.shape)
    first = (lane & (2 * nf - 1)) < nf
    partner = jnp.where(first, pltpu.roll(t, w - nf, axis=1), pltpu.roll(t, nf, axis=1))
    return t * cosw + partner * sinw


def _inproj_kernel(*refs, has_moe):
    if has_moe:
        (x_ref, y_ref, modp_ref, mod_ref, n1_ref, w_ref, cosa_ref, sina_ref, cosc_ref,
         sinc_ref, xo_ref, qa_ref, ka_ref, va_ref, z_ref, xb_ref, qc_ref, kc_ref, vc_ref) = refs
        x = x_ref[0] + modp_ref[0, 5:6, :] * y_ref[0]
        xo_ref[0] = x
    else:
        (x_ref, mod_ref, n1_ref, w_ref, cosa_ref, sina_ref, cosc_ref, sinc_ref,
         qa_ref, ka_ref, va_ref, z_ref, xb_ref, qc_ref, kc_ref, vc_ref) = refs
        x = x_ref[0]
    ms = jnp.mean(x * x, axis=-1, keepdims=True)
    h = x * lax.rsqrt(ms + EPS) * n1_ref[...]
    h = h * (1.0 + mod_ref[0, 1:2, :]) + mod_ref[0, 0:1, :]
    hb = h.astype(BF16)

    def proj(lo, hi):
        return _dot(hb, w_ref[:, lo:hi])

    cosa, sina = cosa_ref[...], sina_ref[...]
    cosc, sinc = cosc_ref[...], sinc_ref[...]
    nfa = A_HEAD_DIM // 4
    nfc = C_QK_DIM // 4
    qa_ref[0] = (_rope(proj(P_AQ, P_AK), cosa, sina, nfa) * (A_HEAD_DIM ** -0.5)).astype(BF16)
    ka_ref[0] = _rope(proj(P_AK, P_AV), cosa, sina, nfa).astype(BF16)
    va_ref[0] = proj(P_AV, P_BZ).astype(BF16)
    z_ref[0] = proj(P_BZ, P_BX)
    xb_ref[0] = proj(P_BX, P_CQ)
    qc_ref[0] = (_rope(proj(P_CQ, P_CK), cosc, sinc, nfc) * (C_QK_DIM ** -0.5)).astype(BF16)
    kc_ref[0] = _rope(proj(P_CK, P_CV), cosc, sinc, nfc).astype(BF16)
    vc_ref[0] = proj(P_CV, P_COLS).astype(BF16)


def _mod_index(nb):
    return lambda b, j: (jnp.where(j == 0, nb, b), 0, 0)


def _inproj(x, moe, mod_prev, mod_l, n1w, w_p, tabs):
    B, S, D = x.shape
    nblk = S // TM
    row = lambda w: pl.BlockSpec((1, TM, w), lambda b, j: (b, j, 0))
    tab = pl.BlockSpec((TM, LANES), lambda b, j: (j, 0))
    modspec = pl.BlockSpec((1, 6, D), _mod_index(B))
    in_specs, args = [row(D)], [x]
    has_moe = moe is not None
    if has_moe:
        in_specs += [row(D), modspec]
        args += [moe, mod_prev]
    in_specs += [modspec, pl.BlockSpec((1, D), lambda b, j: (0, 0)),
                 pl.BlockSpec((D, P_COLS), lambda b, j: (0, 0)), tab, tab, tab, tab]
    args += [mod_l, n1w.reshape(1, D), w_p, *tabs]
    widths = [(256, BF16), (256, BF16), (256, BF16), (512, F32), (896, F32), (256, BF16), (256, BF16),
              (256, BF16)]
    out_specs = [row(w) for w, _ in widths]
    out_shape = [jax.ShapeDtypeStruct((B, S, w), dt) for w, dt in widths]
    if has_moe:
        out_specs = [row(D)] + out_specs
        out_shape = [jax.ShapeDtypeStruct((B, S, D), F32)] + out_shape
    vmem = 2 * D * P_COLS * 2 + 8 * TM * D * 4 + 4 * TM * P_COLS * 4
    return pl.pallas_call(
        functools.partial(_inproj_kernel, has_moe=has_moe),
        grid=(B, nblk), in_specs=in_specs, out_specs=out_specs, out_shape=out_shape,
        compiler_params=_cparams(("parallel", "parallel"), vmem),
        name="inproj",
    )(*args)


def _attn_a_kernel(sink_ref, q_ref, k_ref, v_ref, o_ref, *, lc):
    j = pl.program_id(1)
    S = k_ref.shape[1]
    span = TM + 2 * A_WINDOW
    start = jnp.clip(j * TM - A_WINDOW, 0, S - span)
    start = pl.multiple_of(start, LANES)
    qrow = j * TM + _row_iota((TM, span))
    krow = start + _lane_iota((TM, span))
    mask = (jnp.abs(krow - qrow) <= A_WINDOW) & (krow >= lc) & (j > 0)
    mask2 = jnp.concatenate([mask, mask], axis=0)
    lane = _lane_iota((TM, LANES))
    low = lane < A_HEAD_DIM
    top = _row_iota((2 * TM, 1)) < TM
    for c in range(A_KV_HEADS):
        cs = slice(c * LANES, (c + 1) * LANES)
        q = q_ref[0, :, cs]
        zero = jnp.zeros_like(q)
        qs = jnp.concatenate([jnp.where(low, q, zero), jnp.where(low, zero, q)], axis=0)
        kl = k_ref[0, pl.ds(start, span), cs]
        vl = v_ref[0, pl.ds(start, span), cs]
        kc = k_ref[0, 0:lc, cs]
        vc = v_ref[0, 0:lc, cs]
        s_l = jnp.where(mask2, _dot_nt(qs, kl), NEG)
        s_c = _dot_nt(qs, kc)
        sink = jnp.where(top, sink_ref[2 * c], sink_ref[2 * c + 1])
        m = jnp.maximum(jnp.maximum(jnp.max(s_l, axis=-1, keepdims=True),
                                    jnp.max(s_c, axis=-1, keepdims=True)), sink)
        p_l = jnp.exp(s_l - m)
        p_c = jnp.exp(s_c - m)
        den = (jnp.sum(p_l, axis=-1, keepdims=True) + jnp.sum(p_c, axis=-1, keepdims=True)
               + jnp.exp(sink - m))
        o = (_dot(p_l.astype(BF16), vl) + _dot(p_c.astype(BF16), vc)) * (1.0 / den)
        o_ref[0, :, cs] = jnp.where(low, o[:TM], o[TM:]).astype(BF16)


def _attn_a(sink, qa, ka, va, lc):
    B, S, W = qa.shape
    nblk = S // TM
    blk = pl.BlockSpec((1, TM, W), lambda b, j: (b, j, 0))
    full = pl.BlockSpec((1, S, W), lambda b, j: (b, 0, 0))
    return pl.pallas_call(
        functools.partial(_attn_a_kernel, lc=lc),
        grid=(B, nblk),
        in_specs=[pl.BlockSpec(memory_space=pltpu.SMEM), blk, full, full],
        out_specs=blk,
        out_shape=jax.ShapeDtypeStruct((B, S, W), BF16),
        compiler_params=_cparams(("parallel", "parallel"), 4 * S * W * 2 + 12 * 2 * TM * 768 * 4),
        name="attn_a",
    )(sink, qa, ka, va)


def _attn_c_body(q_ref, k_ref, v_ref, cl_ref, sw_ref, o_ref, nk, lambda_init):
    cl = cl_ref[...]
    lam = (jnp.exp(jnp.sum(cl[0:1] * cl[1:2], axis=-1, keepdims=True))
           - jnp.exp(jnp.sum(cl[2:3] * cl[3:4], axis=-1, keepdims=True)) + lambda_init)
    lane = _lane_iota((CH, LANES))
    seg_r = _row_iota((LANES, LANES)) // C_V_DIM
    seg_c = _lane_iota((LANES, LANES)) // C_V_DIM
    headsum = (seg_r == seg_c).astype(F32)
    for hp in range(C_HEADS // 2):
        cs = slice(hp * LANES, (hp + 1) * LANES)
        q = q_ref[0, :, cs]
        zero = jnp.zeros_like(q)
        k = k_ref[0, 0:nk, cs]
        v = v_ref[0, 0:nk, cs]
        outs = []
        for hh in range(2):
            base = hh * 2 * C_QK_DIM
            in0 = (lane >= base) & (lane < base + C_QK_DIM)
            in1 = (lane >= base + C_QK_DIM) & (lane < base + 2 * C_QK_DIM)
            qs = jnp.concatenate([jnp.where(in0, q, zero), jnp.where(in1, q, zero)], axis=0)
            s = _dot_nt(qs, k)
            m = jnp.max(s, axis=-1, keepdims=True)
            e = jnp.exp(s - m)
            inv = 1.0 / jnp.sum(e, axis=-1, keepdims=True)
            w = e[:CH] * inv[:CH] - e[CH:] * (lam * inv[CH:])
            outs.append(_dot(w.astype(BF16), v))
        o = jnp.where(lane < C_V_DIM, outs[0], outs[1])
        ss = jnp.dot(o * o, headsum, precision=HI, preferred_element_type=F32)
        y = o * lax.rsqrt(ss * (1.0 / C_V_DIM) + EPS) * sw_ref[...] * (1.0 - lambda_init)
        o_ref[0, :, cs] = y.astype(BF16)


def _attn_c_kernel(q_ref, k_ref, v_ref, cl_ref, sw_ref, o_ref, *, lc, lambda_init):
    j = pl.program_id(1)
    S = k_ref.shape[1]

    @pl.when(j < lc // CH)
    def _():
        _attn_c_body(q_ref, k_ref, v_ref, cl_ref, sw_ref, o_ref, lc, lambda_init)

    @pl.when(j >= lc // CH)
    def _():
        _attn_c_body(q_ref, k_ref, v_ref, cl_ref, sw_ref, o_ref, S, lambda_init)


def _attn_c(qc, kc, vc, c_lambda, subln_w, lc, lambda_init):
    B, S, W = qc.shape
    blk = pl.BlockSpec((1, CH, W), lambda b, j: (b, j, 0))
    full = pl.BlockSpec((1, S, W), lambda b, j: (b, 0, 0))
    sw = jnp.tile(subln_w, LANES // C_V_DIM).reshape(1, LANES)
    return pl.pallas_call(
        functools.partial(_attn_c_kernel, lc=lc, lambda_init=lambda_init),
        grid=(B, S // CH),
        in_specs=[blk, full, full, pl.BlockSpec((4, C_QK_DIM), lambda b, j: (0, 0)),
                  pl.BlockSpec((1, LANES), lambda b, j: (0, 0))],
        out_specs=blk,
        out_shape=jax.ShapeDtypeStruct((B, S, W), BF16),
        compiler_params=_cparams(("parallel", "parallel"), 4 * S * W * 2 + 5 * 2 * CH * S * 4),
        name="attn_c",
    )(qc, kc, vc, c_lambda, sw)


XBC_W = B_D_INNER + 2 * B_GROUPS * B_STATE
XB_W = XBC_W + LANES


def _ssd_prep(prev_ref, cur_ref, next_ref, cw_ref, cb_ref, dtb_ref, alog_ref, c, ncc, nch):
    cur = cur_ref[0]
    first = (c == 0) | (c == ncc)
    last = (c == ncc - 1) | (c == nch - 1)
    prev = jnp.where(first, 0.0, prev_ref[0][:, :XBC_W])
    nxt = jnp.where(last, 0.0, next_ref[0][:, :XBC_W])
    ext = jnp.concatenate([prev, cur[:, :XBC_W], nxt], axis=0)
    rows = CH + 2 * SUBLANES
    acc = jnp.zeros((CH, XBC_W), F32) + cb_ref[...]
    for k in range(B_CONV):
        sh = (B_CONV // 2 - k) % rows
        r = ext if sh == 0 else pltpu.roll(ext, sh, axis=0)
        acc = acc + r[SUBLANES:SUBLANES + CH] * cw_ref[k:k + 1, :]
    u = _silu(acc)
    xs = u[:, :B_D_INNER]
    bm = u[:, B_D_INNER:B_D_INNER + LANES]
    cm = u[:, B_D_INNER + LANES:]
    xdt_raw = cur[:, XBC_W:] + dtb_ref[...]
    dt = jnp.maximum(xdt_raw, 0.0) + jnp.log(1.0 + jnp.exp(-jnp.abs(xdt_raw)))
    dta = dt * (-jnp.exp(alog_ref[...]))
    li = _row_iota((CH, CH))
    si = _lane_iota((CH, CH))
    tl = (si <= li).astype(F32)
    tu = (si >= li).astype(F32)
    lane = _lane_iota((CH, LANES))
    cum = jnp.where(lane < B_HEADS,
                    jnp.dot(tl, dta, precision=HI, preferred_element_type=F32),
                    jnp.dot(tu, dta, precision=HI, preferred_element_type=F32))
    er = _row_iota((LANES, B_D_INNER))
    ec = _lane_iota((LANES, B_D_INNER)) // B_HEADDIM
    dtx, cumx = [], []
    for d in range(2):
        e = (er == ec + d * B_HEADS).astype(F32)
        dtx.append(jnp.dot(dt, e, precision=HI, preferred_element_type=F32))
        cumx.append(jnp.dot(cum, e, precision=HI, preferred_element_type=F32))
    aend = [cumx[0][CH - 1:CH, :], cumx[1][0:1, :]]
    return xs, bm, cm, cum, dtx, cumx, aend


def _ssd_specs(S, B):
    n8 = S // SUBLANES
    per = CH // SUBLANES
    prev = pl.BlockSpec((1, SUBLANES, XB_W), lambda b, c: (b, jnp.maximum(c * per - 1, 0), 0))
    cur = pl.BlockSpec((1, CH, XB_W), lambda b, c: (b, c, 0))
    nxt = pl.BlockSpec((1, SUBLANES, XB_W), lambda b, c: (b, jnp.minimum(c * per + per, n8 - 1), 0))
    const = lambda r, w: pl.BlockSpec((r, w), lambda b, c: (0, 0))
    return [prev, cur, nxt, const(B_CONV, XBC_W), const(1, XBC_W), const(1, LANES), const(1, LANES)]


def _ssd_state_kernel(prev_ref, cur_ref, next_ref, cw_ref, cb_ref, dtb_ref, alog_ref, s_ref, da_ref,
                      *, ncc, nch):
    c = pl.program_id(1)
    xs, bm, cm, cum, dtx, cumx, aend = _ssd_prep(prev_ref, cur_ref, next_ref, cw_ref, cb_ref, dtb_ref,
                                                 alog_ref, c, ncc, nch)
    bmt = bm.T.astype(BF16)
    lane = _lane_iota((B_STATE, B_D_INNER))
    for d in range(2):
        xdec = (xs * dtx[d] * jnp.exp(aend[d] - cumx[d])).astype(BF16)
        s2 = _dot(bmt, xdec)
        s_ref[0, 0, d] = jnp.where(lane < B_D_INNER // 2, s2[:B_STATE], s2[B_STATE:])
        da_ref[0, 0, d] = jnp.broadcast_to(jnp.exp(aend[d]), (SUBLANES, B_D_INNER))


def _ssd_scan_kernel(s_ref, da_ref, h_ref, *, ncc, nch):
    order_f = list(range(nch))
    order_b = list(range(ncc - 1, -1, -1)) + list(range(nch - 1, ncc - 1, -1))
    for d, order in enumerate((order_f, order_b)):
        h = jnp.zeros((B_STATE, B_D_INNER), F32)
        for c in order:
            h_ref[0, c, d] = h
            h = da_ref[0, c, d, 0:1, :] * h + s_ref[0, c, d]


def _ssd_out_kernel(prev_ref, cur_ref, next_ref, cw_ref, cb_ref, dtb_ref, alog_ref, z_ref, h_ref, dsk_ref,
                    nw_ref, o_ref, *, ncc, nch):
    c = pl.program_id(1)
    xs, bm, cm, cum, dtx, cumx, aend = _ssd_prep(prev_ref, cur_ref, next_ref, cw_ref, cb_ref, dtb_ref,
                                                 alog_ref, c, ncc, nch)
    half = B_D_INNER // 2
    lane_s = _lane_iota((CH, LANES))
    lane_h = _lane_iota((B_STATE, B_D_INNER))
    lane_y = _lane_iota((CH, half)) // B_HEADDIM
    cmb = cm.astype(BF16)
    bmb = bm.astype(BF16)
    zero = jnp.zeros_like(cmb)
    gmat = [_dot_nt(jnp.where((lane_s // B_STATE) == g, cmb, zero), bmb) for g in range(B_GROUPS)]
    cumt = cum.T
    li = _row_iota((CH, CH))
    si = _lane_iota((CH, CH))
    y = xs * (dsk_ref[0:1, :] + dsk_ref[1:2, :])
    for d in range(2):
        tri = (si <= li) if d == 0 else (si >= li)
        hc = h_ref[0, 0, d]
        h2 = jnp.concatenate([jnp.where(lane_h < half, hc, 0.0), jnp.where(lane_h < half, 0.0, hc)], axis=0)
        y = y + _dot(cmb, h2.astype(BF16)) * jnp.exp(cumx[d])
        xdt = (xs * dtx[d]).astype(BF16)
        parts = []
        for g in range(B_GROUPS):
            acc = jnp.zeros((CH, half), F32)
            for hl in range(B_HEADS // B_GROUPS):
                idx = d * B_HEADS + g * (B_HEADS // B_GROUPS) + hl
                seg = cum[:, idx:idx + 1] - cumt[idx:idx + 1, :]
                dec = jnp.where(tri, jnp.exp(jnp.where(tri, seg, 0.0)), 0.0)
                sc = (gmat[g] * dec).astype(BF16)
                yh = _dot(sc, xdt[:, g * half:(g + 1) * half])
                acc = jnp.where(lane_y == hl, yh, acc)
            parts.append(acc)
        y = y + jnp.concatenate(parts, axis=1)
    gz = y * _silu(z_ref[0])
    outs = []
    for g in range(B_GROUPS):
        gg = gz[:, g * half:(g + 1) * half]
        ms = jnp.mean(gg * gg, axis=-1, keepdims=True)
        outs.append(gg * lax.rsqrt(ms + EPS))
    o_ref[0] = (jnp.concatenate(outs, axis=1) * nw_ref[...]).astype(BF16)


def _ssd(z, xb, conv_w, conv_b, dt_bias, a_log, d_skip, norm_w, lc):
    B, S, _ = z.shape
    nch, ncc = S // CH, lc // CH
    pad = lambda v: jnp.pad(v.reshape(1, -1), ((0, 0), (0, LANES - v.size)))
    consts = [conv_w, conv_b.reshape(1, XBC_W), pad(dt_bias), pad(a_log)]
    specs = _ssd_specs(S, B)
    st_spec = pl.BlockSpec((1, 1, 2, B_STATE, B_D_INNER), lambda b, c: (b, c, 0, 0, 0))
    da_spec = pl.BlockSpec((1, 1, 2, SUBLANES, B_D_INNER), lambda b, c: (b, c, 0, 0, 0))
    st_shape = jax.ShapeDtypeStruct((B, nch, 2, B_STATE, B_D_INNER), F32)
    da_shape = jax.ShapeDtypeStruct((B, nch, 2, SUBLANES, B_D_INNER), F32)
    small = 16 * CH * XB_W * 4
    states, da = pl.pallas_call(
        functools.partial(_ssd_state_kernel, ncc=ncc, nch=nch),
        grid=(B, nch), in_specs=specs, out_specs=[st_spec, da_spec], out_shape=[st_shape, da_shape],
        compiler_params=_cparams(("parallel", "parallel"), small),
        name="ssd_state",
    )(xb, xb, xb, *consts)
    st_bytes = nch * 2 * B_STATE * B_D_INNER * 4
    hstart = pl.pallas_call(
        functools.partial(_ssd_scan_kernel, ncc=ncc, nch=nch),
        grid=(B,),
        in_specs=[pl.BlockSpec((1, nch, 2, B_STATE, B_D_INNER), lambda b: (b, 0, 0, 0, 0)),
                  pl.BlockSpec((1, nch, 2, SUBLANES, B_D_INNER), lambda b: (b, 0, 0, 0, 0))],
        out_specs=pl.BlockSpec((1, nch, 2, B_STATE, B_D_INNER), lambda b: (b, 0, 0, 0, 0)),
        out_shape=st_shape,
        compiler_params=_cparams(("parallel",), 5 * st_bytes // 2),
        name="ssd_scan",
    )(states, da)
    dsk = jnp.repeat(d_skip, B_HEADDIM, axis=1)
    return pl.pallas_call(
        functools.partial(_ssd_out_kernel, ncc=ncc, nch=nch),
        grid=(B, nch),
        in_specs=specs + [pl.BlockSpec((1, CH, B_D_INNER), lambda b, c: (b, c, 0)), st_spec,
                          pl.BlockSpec((2, B_D_INNER), lambda b, c: (0, 0)),
                          pl.BlockSpec((1, B_D_INNER), lambda b, c: (0, 0))],
        out_specs=pl.BlockSpec((1, CH, B_D_INNER), lambda b, c: (b, c, 0)),
        out_shape=jax.ShapeDtypeStruct((B, S, B_D_INNER), BF16),
        compiler_params=_cparams(("parallel", "parallel"), small),
        name="ssd_out",
    )(xb, xb, xb, *consts, z, hstart, dsk, norm_w.reshape(1, B_D_INNER))


def _outproj_kernel(oa_ref, ob_ref, oc_ref, x_ref, mod_ref, n2_ref, w_ref, wr_ref, xo_ref, h2_ref, rt_ref,
                    cnt_ref):
    @pl.when((pl.program_id(0) == 0) & (pl.program_id(1) == 0))
    def _():
        cnt_ref[...] = jnp.zeros_like(cnt_ref)

    a_w = A_HEADS * A_HEAD_DIM
    mix = (_dot(oa_ref[0], w_ref[0:a_w, :]) + _dot(ob_ref[0], w_ref[a_w:a_w + B_D_INNER, :])
           + _dot(oc_ref[0], w_ref[a_w + B_D_INNER:, :]))
    x = x_ref[0] + mod_ref[0, 2:3, :] * mix
    xo_ref[0] = x
    ms = jnp.mean(x * x, axis=-1, keepdims=True)
    h2 = x * lax.rsqrt(ms + EPS) * n2_ref[...]
    h2 = h2 * (1.0 + mod_ref[0, 4:5, :]) + mod_ref[0, 3:4, :]
    h2_ref[0] = h2
    logit = jnp.dot(h2, wr_ref[...], precision=HI, preferred_element_type=F32)
    lf = _lane_iota(logit.shape).astype(F32)
    big = 1e9
    gmask = lf < N_GROUPS
    gl = jnp.where(gmask, logit, NEG)
    gm = jnp.max(gl, axis=-1, keepdims=True)
    g_p = 1.0 / jnp.sum(jnp.exp(gl - gm), axis=-1, keepdims=True)
    g_sel = jnp.min(jnp.where(gmask & (gl == gm), lf, big), axis=-1, keepdims=True)
    lo = N_GROUPS + EXPERTS_PER_GROUP * g_sel
    emask = (lf >= lo) & (lf < lo + EXPERTS_PER_GROUP)
    el = jnp.where(emask, logit, NEG)
    v1 = jnp.max(el, axis=-1, keepdims=True)
    i1 = jnp.min(jnp.where(emask & (el == v1), lf, big), axis=-1, keepdims=True)
    rest = emask & (lf != i1)
    el2 = jnp.where(rest, logit, NEG)
    v2 = jnp.max(el2, axis=-1, keepdims=True)
    i2 = jnp.min(jnp.where(rest & (el2 == v2), lf, big), axis=-1, keepdims=True)
    t = jnp.exp(v2 - v1)
    gate1 = g_p / (1.0 + t)
    gate2 = g_p * t / (1.0 + t)
    oh1 = lf == i1 - N_GROUPS
    oh2 = lf == i2 - N_GROUPS
    oh = jnp.where(oh1 | oh2, 1.0, 0.0)
    earlier = (_lane_iota((TM, TM)) < _row_iota((TM, TM))).astype(BF16)
    prefix = _dot(earlier, oh.astype(BF16)) + cnt_ref[0:1, :]
    rank1 = jnp.sum(jnp.where(oh1, prefix, 0.0), axis=-1, keepdims=True)
    rank2 = jnp.sum(jnp.where(oh2, prefix, 0.0), axis=-1, keepdims=True)
    cnt_ref[...] = cnt_ref[...] + jnp.sum(oh, axis=0, keepdims=True)
    info = jnp.zeros_like(logit)
    for k, v in enumerate((i1 - N_GROUPS, i2 - N_GROUPS, gate1, gate2, rank1, rank2)):
        info = jnp.where(lf == k, v, info)
    rt_ref[0] = info


RT_E, RT_GATE, RT_RANK = 0, 2, 4


def _outproj(oa, ob, oc, x, mod_l, n2w, w_out, w_r):
    B, S, D = x.shape
    row = lambda w: pl.BlockSpec((1, TM, w), lambda b, j: (b, j, 0))
    mixw = w_out.shape[0]
    return pl.pallas_call(
        _outproj_kernel,
        grid=(B, S // TM),
        in_specs=[row(oa.shape[2]), row(ob.shape[2]), row(oc.shape[2]), row(D),
                  pl.BlockSpec((1, 6, D), _mod_index(B)), pl.BlockSpec((1, D), lambda b, j: (0, 0)),
                  pl.BlockSpec((mixw, D), lambda b, j: (0, 0)), pl.BlockSpec((D, LANES), lambda b, j: (0, 0))],
        out_specs=[row(D), row(D), row(LANES), pl.BlockSpec((SUBLANES, LANES), lambda b, j: (0, 0))],
        out_shape=[jax.ShapeDtypeStruct((B, S, D), F32), jax.ShapeDtypeStruct((B, S, D), F32),
                   jax.ShapeDtypeStruct((B, S, LANES), F32), jax.ShapeDtypeStruct((SUBLANES, LANES), F32)],
        compiler_params=_cparams(("arbitrary", "arbitrary"), 2 * mixw * D * 2 + 14 * TM * D * 4),
        name="outproj",
    )(oa, ob, oc, x, mod_l, n2w.reshape(1, D), w_out, w_r)


def _expert_kernel(be_ref, nu_ref, x_ref, wg_ref, wu_ref, wd_ref, o_ref):
    i = pl.program_id(0)

    @pl.when(i < nu_ref[0])
    def _():
        xb = x_ref[...].astype(BF16)
        hid = _silu(_dot(xb, wg_ref[0].astype(BF16))) * _dot(xb, wu_ref[0].astype(BF16))
        o_ref[...] = _dot(hid.astype(BF16), wd_ref[0].astype(BF16))

    @pl.when(i >= nu_ref[0])
    def _():
        o_ref[...] = jnp.zeros_like(o_ref)


def _expert_mlp(buf, block_e, n_used, w_gate, w_up, w_down):
    rows, D = buf.shape
    de = w_gate.shape[2]
    nblk = rows // MOE_ROWS
    grid_spec = pltpu.PrefetchScalarGridSpec(
        num_scalar_prefetch=2, grid=(nblk,),
        in_specs=[pl.BlockSpec((MOE_ROWS, D), lambda i, be, nu: (i, 0)),
                  pl.BlockSpec((1, D, de), lambda i, be, nu: (be[i], 0, 0)),
                  pl.BlockSpec((1, D, de), lambda i, be, nu: (be[i], 0, 0)),
                  pl.BlockSpec((1, de, D), lambda i, be, nu: (be[i], 0, 0))],
        out_specs=pl.BlockSpec((MOE_ROWS, D), lambda i, be, nu: (i, 0)))
    return pl.pallas_call(
        _expert_kernel, grid_spec=grid_spec,
        out_shape=jax.ShapeDtypeStruct((rows, D), F32),
        compiler_params=_cparams(("arbitrary",), 2 * 3 * D * de * 4 + 3 * D * de * 2 + 8 * MOE_ROWS * D * 4),
        name="expert_mlp",
    )(block_e, n_used, buf, w_gate, w_up, w_down)


def _dispatch_plan(route, counts):
    T = route.shape[0]
    n_blocks = -(-T * TOP_K // MOE_ROWS) + N_EXPERTS
    cnt = counts[0, :N_EXPERTS].astype(jnp.int32)
    nb = (cnt + MOE_ROWS - 1) // MOE_ROWS
    blk_end = jnp.cumsum(nb)
    row_start = (blk_end - nb) * MOE_ROWS
    e = route[:, RT_E:RT_E + TOP_K].astype(jnp.int32)
    rank = route[:, RT_RANK:RT_RANK + TOP_K].astype(jnp.int32)
    table = jnp.where(e[:, :, None] == jnp.arange(N_EXPERTS)[None, None, :], row_start[None, None, :], 0)
    pos = (jnp.sum(table, axis=-1) + rank).reshape(T // TM, 1, TM * TOP_K)
    block_e = jnp.sum(blk_end[None, :] <= jnp.arange(n_blocks)[:, None], axis=1)
    block_e = jnp.minimum(block_e, N_EXPERTS - 1).astype(jnp.int32)
    n_used = blk_end[-1].astype(jnp.int32).reshape(1)
    return pos.astype(jnp.int32), block_e, n_used, n_blocks


def _row_copy(src, i, dst, j, sem):
    return pltpu.make_async_copy(src.at[pl.ds(i, 1)], dst.at[pl.ds(j, 1)], sem)


def _dispatch_kernel(pos_ref, x_ref, init_ref, buf_ref, sem):
    del init_ref

    def issue(i, carry):
        for k in range(TOP_K):
            _row_copy(x_ref, i, buf_ref, pos_ref[0, 0, TOP_K * i + k], sem).start()
        return carry

    def drain(i, carry):
        for k in range(TOP_K):
            _row_copy(x_ref, i, buf_ref, pos_ref[0, 0, TOP_K * i + k], sem).wait()
        return carry

    lax.fori_loop(0, TM, issue, 0)
    lax.fori_loop(0, TM, drain, 0)


def _dispatch(h2, pos, n_blocks):
    T, D = h2.shape
    init = jnp.zeros((n_blocks * MOE_ROWS, D), F32)
    return pl.pallas_call(
        _dispatch_kernel,
        grid=(T // TM,),
        in_specs=[pl.BlockSpec((1, 1, TM * TOP_K), lambda i: (i, 0, 0), memory_space=pltpu.SMEM),
                  pl.BlockSpec((TM, D), lambda i: (i, 0)),
                  pl.BlockSpec(memory_space=pl.ANY)],
        out_specs=pl.BlockSpec(memory_space=pl.ANY),
        out_shape=jax.ShapeDtypeStruct(init.shape, F32),
        scratch_shapes=[pltpu.SemaphoreType.DMA],
        input_output_aliases={2: 0},
        compiler_params=_cparams(("arbitrary",), 4 * TM * D * 4),
        name="moe_dispatch",
    )(pos, h2, init)


def _combine_kernel(pos_ref, rt_ref, eo_ref, y_ref, rows_ref, sem):
    def issue(i, carry):
        for k in range(TOP_K):
            _row_copy(eo_ref, pos_ref[0, 0, TOP_K * i + k], rows_ref.at[k], i, sem).start()
        return carry

    def drain(i, carry):
        for k in range(TOP_K):
            _row_copy(eo_ref, pos_ref[0, 0, TOP_K * i + k], rows_ref.at[k], i, sem).wait()
        return carry

    lax.fori_loop(0, TM, issue, 0)
    lax.fori_loop(0, TM, drain, 0)
    rt = rt_ref[...]
    y_ref[...] = (rt[:, RT_GATE:RT_GATE + 1] * rows_ref[0] + rt[:, RT_GATE + 1:RT_GATE + 2] * rows_ref[1])


def _combine(eo, pos, route):
    T = route.shape[0]
    D = eo.shape[1]
    return pl.pallas_call(
        _combine_kernel,
        grid=(T // TM,),
        in_specs=[pl.BlockSpec((1, 1, TM * TOP_K), lambda i: (i, 0, 0), memory_space=pltpu.SMEM),
                  pl.BlockSpec((TM, LANES), lambda i: (i, 0)),
                  pl.BlockSpec(memory_space=pl.ANY)],
        out_specs=pl.BlockSpec((TM, D), lambda i: (i, 0)),
        out_shape=jax.ShapeDtypeStruct((T, D), F32),
        scratch_shapes=[pltpu.VMEM((TOP_K, TM, D), F32), pltpu.SemaphoreType.DMA],
        compiler_params=_cparams(("arbitrary",), 6 * TM * D * 4),
        name="moe_combine",
    )(pos, route, eo)


def _final_kernel(x_ref, y_ref, modp_ref, nw_ref, o_ref):
    x = x_ref[0] + modp_ref[0, 5:6, :] * y_ref[0]
    ms = jnp.mean(x * x, axis=-1, keepdims=True)
    o_ref[0] = x * lax.rsqrt(ms + EPS) * nw_ref[...]


def _final(x, y, mod_prev, nw, lc):
    B, S, D = x.shape
    off = lc // TM
    row = lambda w: pl.BlockSpec((1, TM, w), lambda b, j: (b, j + off, 0))
    return pl.pallas_call(
        _final_kernel,
        grid=(B, (S - lc) // TM),
        in_specs=[row(D), row(D), pl.BlockSpec((1, 6, D), lambda b, j: (b, 0, 0)),
                  pl.BlockSpec((1, D), lambda b, j: (0, 0))],
        out_specs=pl.BlockSpec((1, TM, D), lambda b, j: (b, j, 0)),
        out_shape=jax.ShapeDtypeStruct((B, S - lc, D), F32),
        compiler_params=_cparams(("parallel", "parallel"), 8 * TM * D * 4),
        name="final_norm",
    )(x, y, mod_prev, nw.reshape(1, D))


def _pack_w_in(w):
    d = w.shape[0]
    aq, ak, av = w[:, 0:256], w[:, 256:384], w[:, 384:512]
    dup = lambda m: jnp.concatenate([m[:, 0:64], m[:, 0:64], m[:, 64:128], m[:, 64:128]], axis=1)
    bz, bx, bdt = w[:, 512:1024], w[:, 1024:1792], w[:, 1792:1808]
    cq, ck, cv = w[:, 1808:2064], w[:, 2064:2320], w[:, 2320:2576]
    packed = jnp.concatenate([aq, dup(ak), dup(av), bz, bx, bdt, jnp.zeros((d, LANES - 16), w.dtype),
                              cq, ck, cv], axis=1)
    return packed.astype(BF16)


def kernel(x, c, ctx, c_ctx, w_mod, b_mod, norm1_w, norm2_w, w_in, w_out, a_sink, b_conv_w, b_conv_b,
           b_dt_bias, b_a_log, b_d, b_norm_w, c_lambda, c_subln_w, moe_group_router, moe_router,
           moe_w_gate, moe_w_up, moe_w_down, final_norm_w):
    B, L, D = x.shape
    Lc = ctx.shape[1]
    depth = w_mod.shape[0]
    assert Lc == TM and L % TM == 0 and B + 1 <= SUBLANES
    S = Lc + L
    T = B * S
    xa = jnp.concatenate([ctx, x], axis=1)
    c_all = jnp.concatenate([c, c_ctx[None, :], jnp.zeros((SUBLANES - B - 1, D), F32)], axis=0)
    mod = _mod_table(c_all, w_mod, b_mod)
    tabs = _rope_tables(L, Lc, A_HEAD_DIM) + _rope_tables(L, Lc, C_QK_DIM)
    moe, mod_prev = None, None
    for l in range(depth):
        mod_l = mod[l].reshape(SUBLANES, 6, D)
        lambda_init = 0.8 - 0.6 * math.exp(-0.3 * l)
        outs = _inproj(xa, moe, mod_prev, mod_l, norm1_w[l], _pack_w_in(w_in[l]), tabs)
        if moe is not None:
            xa, outs = outs[0], outs[1:]
        qa, ka, va, z, xb, qc, kc, vc = outs
        oa = _attn_a(a_sink[l], qa, ka, va, Lc)
        ob = _ssd(z, xb, b_conv_w[l], b_conv_b[l], b_dt_bias[l], b_a_log[l], b_d[l], b_norm_w[l], Lc)
        oc = _attn_c(qc, kc, vc, c_lambda[l], c_subln_w[l], Lc, lambda_init)
        w_r = jnp.concatenate([moe_group_router[l], moe_router[l],
                               jnp.zeros((D, LANES - N_GROUPS - N_EXPERTS), F32)], axis=1)
        xa, h2, route, counts = _outproj(oa, ob, oc, xa, mod_l, norm2_w[l], w_out[l].astype(BF16), w_r)
        route = route.reshape(T, LANES)
        pos, block_e, n_used, n_blocks = _dispatch_plan(route, counts)
        buf = _dispatch(h2.reshape(T, D), pos, n_blocks)
        eo = _expert_mlp(buf, block_e, n_used, moe_w_gate[l], moe_w_up[l], moe_w_down[l])
        moe, mod_prev = _combine(eo, pos, route).reshape(B, S, D), mod_l
    return _final(xa, moe, mod_prev, final_norm_w, Lc)
```

```python
import functools
import math

import jax
import jax.numpy as jnp
from jax import lax
from jax.experimental import pallas as pl
from jax.experimental.pallas import tpu as pltpu

F32 = jnp.float32
BF16 = jnp.bfloat16
HI = lax.Precision.HIGHEST

GRID_W = 64
EPS = 1e-6
ROPE_BASE = 10000.0

A_HEADS = 4
A_KV_HEADS = 2
A_HEAD_DIM = 64
A_WINDOW = 128
B_D_INNER = 512
B_HEADDIM = 64
B_HEADS = B_D_INNER // B_HEADDIM
B_GROUPS = 2
B_STATE = 64
B_CONV = 5
C_HEADS = 4
C_QK_DIM = 32
C_V_DIM = 64
N_GROUPS = 4
EXPERTS_PER_GROUP = 8
N_EXPERTS = N_GROUPS * EXPERTS_PER_GROUP
TOP_K = 2

LANES = 128
SUBLANES = 8
TM = 256
CH = 128
MOE_ROWS = 512
DMA_UNROLL = 8
NEG = -1e30
VMEM_CAP = 64 * 1024 * 1024

P_AQ = 0
P_AK = 256
P_AV = 512
P_BZ = 768
P_BX = 1280
P_BDT = 2048
P_CQ = 2176
P_CK = 2432
P_CV = 2688
P_COLS = 2944


def _cparams(sem, vmem_bytes):
    limit = int(min(max(2 * vmem_bytes, 16 * 1024 * 1024), VMEM_CAP - 8 * 1024 * 1024))
    return pltpu.CompilerParams(dimension_semantics=sem, vmem_limit_bytes=limit)


def _lane_iota(shape):
    return lax.broadcasted_iota(jnp.int32, shape, len(shape) - 1)


def _row_iota(shape):
    return lax.broadcasted_iota(jnp.int32, shape, len(shape) - 2)


def _silu(v):
    return v * (1.0 / (1.0 + jnp.exp(-v)))


def _dot(a, b):
    return jnp.dot(a, b, preferred_element_type=F32)


def _dot_nt(a, b):
    return lax.dot_general(a, b, (((1,), (1,)), ((), ())), preferred_element_type=F32)


def _mod_kernel(c_ref, w_ref, b_ref, o_ref):
    s = _silu(c_ref[...]).astype(BF16)
    o_ref[0] = _dot(s, w_ref[0].astype(BF16)) + b_ref[0]


def _mod_table(c_all, w_mod, b_mod):
    depth, d, d6 = w_mod.shape
    tn = 1536
    return pl.pallas_call(
        _mod_kernel,
        grid=(depth, d6 // tn),
        in_specs=[
            pl.BlockSpec((SUBLANES, d), lambda l, n: (0, 0)),
            pl.BlockSpec((1, d, tn), lambda l, n: (l, 0, n)),
            pl.BlockSpec((1, 1, tn), lambda l, n: (l, 0, n)),
        ],
        out_specs=pl.BlockSpec((1, SUBLANES, tn), lambda l, n: (l, 0, n)),
        out_shape=jax.ShapeDtypeStruct((depth, SUBLANES, d6), F32),
        compiler_params=_cparams(("parallel", "parallel"), 2 * d * tn * 4 + d * tn * 2),
        name="mod_table",
    )(c_all, w_mod, b_mod.reshape(depth, 1, d6))


def _rope_tables(L, Lc, dim):
    half = dim // 2
    nf = half // 2
    rows = L // GRID_W
    row_pos = jnp.repeat(jnp.arange(rows, dtype=jnp.int32), GRID_W).astype(F32)
    col_pos = jnp.tile(jnp.arange(GRID_W, dtype=jnp.int32), rows).astype(F32)
    freqs = ROPE_BASE ** (-jnp.arange(nf, dtype=F32) / nf)
    d = jnp.arange(dim)
    within = d % half
    f = freqs[within % nf]
    first = within < nf
    pos = jnp.where((d // half == 0)[None, :], row_pos[:, None], col_pos[:, None])
    ang = pos * f[None, :]
    cos = jnp.cos(ang)
    sin = jnp.where(first[None, :], -jnp.sin(ang), jnp.sin(ang))
    cos = jnp.concatenate([jnp.ones((Lc, dim), F32), cos], axis=0)
    sin = jnp.concatenate([jnp.zeros((Lc, dim), F32), sin], axis=0)
    reps = LANES // dim
    return jnp.tile(cos, (1, reps)), jnp.tile(sin, (1, reps))


def _rope(t, cos, sin, nf):
    w = t.shape[1]
    reps = w // LANES
    cosw = jnp.concatenate([cos] * reps, axis=1) if reps > 1 else cos
    sinw = jnp.concatenate([sin] * reps, axis=1) if reps > 1 else sin
    lane = _lane_iota(t.shape)
    first = (lane & (2 * nf - 1)) < nf
    partner = jnp.where(first, pltpu.roll(t, w - nf, axis=1), pltpu.roll(t, nf, axis=1))
    return t * cosw + partner * sinw


def _inproj_kernel(*refs, has_moe):
    if has_moe:
        (x_ref, y_ref, modp_ref, mod_ref, n1_ref, w_ref, cosa_ref, sina_ref, cosc_ref,
         sinc_ref, xo_ref, qa_ref, ka_ref, va_ref, z_ref, xb_ref, qc_ref, kc_ref, vc_ref) = refs
        x = x_ref[0] + modp_ref[0, 5:6, :] * y_ref[0]
        xo_ref[0] = x
    else:
        (x_ref, mod_ref, n1_ref, w_ref, cosa_ref, sina_ref, cosc_ref, sinc_ref,
         qa_ref, ka_ref, va_ref, z_ref, xb_ref, qc_ref, kc_ref, vc_ref) = refs
        x = x_ref[0]
    ms = jnp.mean(x * x, axis=-1, keepdims=True)
    h = x * lax.rsqrt(ms + EPS) * n1_ref[...]
    h = h * (1.0 + mod_ref[0, 1:2, :]) + mod_ref[0, 0:1, :]
    hb = h.astype(BF16)

    def proj(lo, hi):
        return _dot(hb, w_ref[:, lo:hi])

    cosa, sina = cosa_ref[...], sina_ref[...]
    cosc, sinc = cosc_ref[...], sinc_ref[...]
    nfa = A_HEAD_DIM // 4
    nfc = C_QK_DIM // 4
    qa_ref[0] = (_rope(proj(P_AQ, P_AK), cosa, sina, nfa) * (A_HEAD_DIM ** -0.5)).astype(BF16)
    ka_ref[0] = _rope(proj(P_AK, P_AV), cosa, sina, nfa).astype(BF16)
    va_ref[0] = proj(P_AV, P_BZ).astype(BF16)
    z_ref[0] = proj(P_BZ, P_BX)
    xb_ref[0] = proj(P_BX, P_CQ)
    qc_ref[0] = (_rope(proj(P_CQ, P_CK), cosc, sinc, nfc) * (C_QK_DIM ** -0.5)).astype(BF16)
    kc_ref[0] = _rope(proj(P_CK, P_CV), cosc, sinc, nfc).astype(BF16)
    vc_ref[0] = proj(P_CV, P_COLS).astype(BF16)


def _mod_index(nb):
    return lambda b, j: (jnp.where(j == 0, nb, b), 0, 0)


def _inproj(x, moe, mod_prev, mod_l, n1w, w_p, tabs):
    B, S, D = x.shape
    nblk = S // TM
    row = lambda w: pl.BlockSpec((1, TM, w), lambda b, j: (b, j, 0))
    tab = pl.BlockSpec((TM, LANES), lambda b, j: (j, 0))
    modspec = pl.BlockSpec((1, 6, D), _mod_index(B))
    in_specs, args = [row(D)], [x]
    has_moe = moe is not None
    if has_moe:
        in_specs += [row(D), modspec]
        args += [moe, mod_prev]
    in_specs += [modspec, pl.BlockSpec((1, D), lambda b, j: (0, 0)),
                 pl.BlockSpec((D, P_COLS), lambda b, j: (0, 0)), tab, tab, tab, tab]
    args += [mod_l, n1w.reshape(1, D), w_p, *tabs]
    widths = [(256, BF16), (256, BF16), (256, BF16), (512, F32), (896, F32), (256, BF16), (256, BF16),
              (256, BF16)]
    out_specs = [row(w) for w, _ in widths]
    out_shape = [jax.ShapeDtypeStruct((B, S, w), dt) for w, dt in widths]
    if has_moe:
        out_specs = [row(D)] + out_specs
        out_shape = [jax.ShapeDtypeStruct((B, S, D), F32)] + out_shape
    vmem = 2 * D * P_COLS * 2 + 8 * TM * D * 4 + 4 * TM * P_COLS * 4
    return pl.pallas_call(
        functools.partial(_inproj_kernel, has_moe=has_moe),
        grid=(B, nblk), in_specs=in_specs, out_specs=out_specs, out_shape=out_shape,
        compiler_params=_cparams(("parallel", "parallel"), vmem),
        name="inproj",
    )(*args)


def _attn_a_kernel(sink_ref, q_ref, k_ref, v_ref, o_ref, *, lc):
    j = pl.program_id(1)
    S = k_ref.shape[1]
    span = TM + 2 * A_WINDOW
    start = jnp.clip(j * TM - A_WINDOW, 0, S - span)
    start = pl.multiple_of(start, LANES)
    qrow = j * TM + _row_iota((TM, span))
    krow = start + _lane_iota((TM, span))
    mask = (jnp.abs(krow - qrow) <= A_WINDOW) & (krow >= lc) & (j > 0)
    mask2 = jnp.concatenate([mask, mask], axis=0)
    lane = _lane_iota((TM, LANES))
    low = lane < A_HEAD_DIM
    top = _row_iota((2 * TM, 1)) < TM
    for c in range(A_KV_HEADS):
        cs = slice(c * LANES, (c + 1) * LANES)
        q = q_ref[0, :, cs]
        zero = jnp.zeros_like(q)
        qs = jnp.concatenate([jnp.where(low, q, zero), jnp.where(low, zero, q)], axis=0)
        kl = k_ref[0, pl.ds(start, span), cs]
        vl = v_ref[0, pl.ds(start, span), cs]
        kc = k_ref[0, 0:lc, cs]
        vc = v_ref[0, 0:lc, cs]
        s_l = jnp.where(mask2, _dot_nt(qs, kl), NEG)
        s_c = _dot_nt(qs, kc)
        sink = jnp.where(top, sink_ref[2 * c], sink_ref[2 * c + 1])
        m = jnp.maximum(jnp.maximum(jnp.max(s_l, axis=-1, keepdims=True),
                                    jnp.max(s_c, axis=-1, keepdims=True)), sink)
        p_l = jnp.exp(s_l - m)
        p_c = jnp.exp(s_c - m)
        den = (jnp.sum(p_l, axis=-1, keepdims=True) + jnp.sum(p_c, axis=-1, keepdims=True)
               + jnp.exp(sink - m))
        o = (_dot(p_l.astype(BF16), vl) + _dot(p_c.astype(BF16), vc)) * (1.0 / den)
        o_ref[0, :, cs] = jnp.where(low, o[:TM], o[TM:]).astype(BF16)


def _attn_a(sink, qa, ka, va, lc):
    B, S, W = qa.shape
    nblk = S // TM
    blk = pl.BlockSpec((1, TM, W), lambda b, j: (b, j, 0))
    full = pl.BlockSpec((1, S, W), lambda b, j: (b, 0, 0))
    return pl.pallas_call(
        functools.partial(_attn_a_kernel, lc=lc),
        grid=(B, nblk),
        in_specs=[pl.BlockSpec(memory_space=pltpu.SMEM), blk, full, full],
        out_specs=blk,
        out_shape=jax.ShapeDtypeStruct((B, S, W), BF16),
        compiler_params=_cparams(("parallel", "parallel"), 4 * S * W * 2 + 12 * 2 * TM * 768 * 4),
        name="attn_a",
    )(sink, qa, ka, va)


def _split3(x):
    hi = x.astype(BF16)
    r = x - hi.astype(F32)
    mid = r.astype(BF16)
    lo = (r - mid.astype(F32)).astype(BF16)
    return hi, mid, lo


def _dot_f32_by_01(x, e01):
    return _dot(jnp.concatenate(_split3(x), axis=1), jnp.concatenate([e01] * 3, axis=0))


def _dot_01_by_f32(t01, x):
    n = x.shape[1]
    r = _dot(t01, jnp.concatenate(_split3(x), axis=1))
    return r[:, :n] + r[:, n:2 * n] + r[:, 2 * n:]


KEY_CHUNK = 512


def _attn_c_body(q_ref, k_ref, v_ref, cl_ref, sw_ref, o_ref, s_ref, chunks, lambda_init):
    cl = cl_ref[...]
    lam = (jnp.exp(jnp.sum(cl[0:1] * cl[1:2], axis=-1, keepdims=True))
           - jnp.exp(jnp.sum(cl[2:3] * cl[3:4], axis=-1, keepdims=True)) + lambda_init)
    lane = _lane_iota((TM, LANES))
    seg_r = _row_iota((LANES, LANES)) // C_V_DIM
    seg_c = _lane_iota((LANES, LANES)) // C_V_DIM
    headsum = (seg_r == seg_c).astype(BF16)
    lane2 = _lane_iota((2 * TM, LANES))
    mrun, m, acc, qs, outs = None, None, None, None, []
    for t in range(C_HEADS + 1):
        if t >= 1:
            php, phh, pslot = (t - 1) // 2, (t - 1) % 2, (t - 1) % 2
            pcs = slice(php * LANES, (php + 1) * LANES)
            m = jnp.max(mrun, axis=-1, keepdims=True)
            acc = jnp.zeros((2 * TM, LANES), F32)
        if t < C_HEADS:
            hp, hh, slot = t // 2, t % 2, t % 2
            cs = slice(hp * LANES, (hp + 1) * LANES)
            q = q_ref[0, :, cs]
            zero = jnp.zeros_like(q)
            base = hh * 2 * C_QK_DIM
            in0 = (lane >= base) & (lane < base + C_QK_DIM)
            in1 = (lane >= base + C_QK_DIM) & (lane < base + 2 * C_QK_DIM)
            qs = jnp.concatenate([jnp.where(in0, q, zero), jnp.where(in1, q, zero)], axis=0)
            mrun = jnp.full((2 * TM, LANES), NEG, F32)
        for st, sz in chunks:
            if t < C_HEADS:
                s = _dot_nt(qs, k_ref[0, st:st + sz, cs])
                s_ref[slot, :, st:st + sz] = s
                for u in range(sz // LANES):
                    mrun = jnp.maximum(mrun, s[:, u * LANES:(u + 1) * LANES])
            if t >= 1:
                e = jnp.exp(s_ref[pslot, :, st:st + sz] - m).astype(BF16)
                vs = v_ref[0, st:st + sz, pcs]
                own = (_lane_iota(vs.shape) // C_V_DIM) == phh
                acc = acc + _dot(e, jnp.where(own, vs, jnp.ones_like(vs)))
        if t >= 1:
            own_o = (lane2 // C_V_DIM) == phh
            on = acc / jnp.where(own_o, pltpu.roll(acc, C_V_DIM, axis=1), 1.0)
            outs.append(on[:TM] - lam * on[TM:])
            if phh == 1:
                o = jnp.where(lane < C_V_DIM, outs[0], outs[1])
                ss = _dot_f32_by_01(o * o, headsum)
                y = o * lax.rsqrt(ss * (1.0 / C_V_DIM) + EPS) * sw_ref[...] * (1.0 - lambda_init)
                o_ref[0, :, pcs] = y.astype(BF16)
                outs = []


def _attn_c_kernel(q_ref, k_ref, v_ref, cl_ref, sw_ref, o_ref, s_ref, *, lc, lambda_init):
    j = pl.program_id(1)
    S = k_ref.shape[1]
    ctx_chunks = [(0, lc)]
    all_chunks = ctx_chunks + [(st, KEY_CHUNK) for st in range(lc, S, KEY_CHUNK)]

    @pl.when(j == 0)
    def _():
        _attn_c_body(q_ref, k_ref, v_ref, cl_ref, sw_ref, o_ref, s_ref, ctx_chunks, lambda_init)

    @pl.when(j > 0)
    def _():
        _attn_c_body(q_ref, k_ref, v_ref, cl_ref, sw_ref, o_ref, s_ref, all_chunks, lambda_init)


def _attn_c(qc, kc, vc, c_lambda, subln_w, lc, lambda_init):
    B, S, W = qc.shape
    assert lc == TM and (S - lc) % KEY_CHUNK == 0
    blk = pl.BlockSpec((1, TM, W), lambda b, j: (b, j, 0))
    full = pl.BlockSpec((1, S, W), lambda b, j: (b, 0, 0))
    sw = jnp.tile(subln_w, LANES // C_V_DIM).reshape(1, LANES)
    return pl.pallas_call(
        functools.partial(_attn_c_kernel, lc=lc, lambda_init=lambda_init),
        grid=(B, S // TM),
        in_specs=[blk, full, full, pl.BlockSpec((4, C_QK_DIM), lambda b, j: (0, 0)),
                  pl.BlockSpec((1, LANES), lambda b, j: (0, 0))],
        out_specs=blk,
        out_shape=jax.ShapeDtypeStruct((B, S, W), BF16),
        scratch_shapes=[pltpu.VMEM((2, 2 * TM, S), F32)],
        compiler_params=_cparams(("parallel", "parallel"),
                                 4 * S * W * 2 + 2 * 2 * TM * S * 4 + 6 * 2 * TM * KEY_CHUNK * 4),
        name="attn_c",
    )(qc, kc, vc, c_lambda, sw)


XBC_W = B_D_INNER + 2 * B_GROUPS * B_STATE
XB_W = XBC_W + LANES


def _ssd_prep(prev_ref, cur_ref, next_ref, cw_ref, cb_ref, dtb_ref, alog_ref, c, ncc, nch):
    cur = cur_ref[0]
    first = (c == 0) | (c == ncc)
    last = (c == ncc - 1) | (c == nch - 1)
    prev = jnp.where(first, 0.0, prev_ref[0][:, :XBC_W])
    nxt = jnp.where(last, 0.0, next_ref[0][:, :XBC_W])
    ext = jnp.concatenate([prev, cur[:, :XBC_W], nxt], axis=0)
    rows = CH + 2 * SUBLANES
    acc = jnp.zeros((CH, XBC_W), F32) + cb_ref[...]
    for k in range(B_CONV):
        sh = (B_CONV // 2 - k) % rows
        r = ext if sh == 0 else pltpu.roll(ext, sh, axis=0)
        acc = acc + r[SUBLANES:SUBLANES + CH] * cw_ref[k:k + 1, :]
    u = _silu(acc)
    xs = u[:, :B_D_INNER]
    bm = u[:, B_D_INNER:B_D_INNER + LANES]
    cm = u[:, B_D_INNER + LANES:]
    xdt_raw = cur[:, XBC_W:] + dtb_ref[...]
    dt = jnp.maximum(xdt_raw, 0.0) + jnp.log(1.0 + jnp.exp(-jnp.abs(xdt_raw)))
    dta = dt * (-jnp.exp(alog_ref[...]))
    li = _row_iota((CH, CH))
    si = _lane_iota((CH, CH))
    tl = (si <= li).astype(BF16)
    tu = (si >= li).astype(BF16)
    lane = _lane_iota((CH, LANES))
    cum = jnp.where(lane < B_HEADS, _dot_01_by_f32(tl, dta), _dot_01_by_f32(tu, dta))
    er = _row_iota((LANES, B_D_INNER))
    ec = _lane_iota((LANES, B_D_INNER)) // B_HEADDIM
    dtx, cumx = [], []
    for d in range(2):
        e = (er == ec + d * B_HEADS).astype(BF16)
        dtx.append(_dot_f32_by_01(dt, e))
        cumx.append(_dot_f32_by_01(cum, e))
    aend = [cumx[0][CH - 1:CH, :], cumx[1][0:1, :]]
    return xs, bm, cm, cum, dtx, cumx, aend


def _ssd_specs(S, B):
    n8 = S // SUBLANES
    per = CH // SUBLANES
    prev = pl.BlockSpec((1, SUBLANES, XB_W), lambda b, c: (b, jnp.maximum(c * per - 1, 0), 0))
    cur = pl.BlockSpec((1, CH, XB_W), lambda b, c: (b, c, 0))
    nxt = pl.BlockSpec((1, SUBLANES, XB_W), lambda b, c: (b, jnp.minimum(c * per + per, n8 - 1), 0))
    const = lambda r, w: pl.BlockSpec((r, w), lambda b, c: (0, 0))
    return [prev, cur, nxt, const(B_CONV, XBC_W), const(1, XBC_W), const(1, LANES), const(1, LANES)]


def _ssd_state_kernel(prev_ref, cur_ref, next_ref, cw_ref, cb_ref, dtb_ref, alog_ref, s_ref, da_ref,
                      *, ncc, nch):
    c = pl.program_id(1)
    xs, bm, cm, cum, dtx, cumx, aend = _ssd_prep(prev_ref, cur_ref, next_ref, cw_ref, cb_ref, dtb_ref,
                                                 alog_ref, c, ncc, nch)
    bmt = bm.T.astype(BF16)
    lane = _lane_iota((B_STATE, B_D_INNER))
    for d in range(2):
        xdec = (xs * dtx[d] * jnp.exp(aend[d] - cumx[d])).astype(BF16)
        s2 = _dot(bmt, xdec)
        s_ref[0, 0, d] = jnp.where(lane < B_D_INNER // 2, s2[:B_STATE], s2[B_STATE:])
        da_ref[0, 0, d] = jnp.broadcast_to(jnp.exp(aend[d]), (SUBLANES, B_D_INNER))


def _ssd_scan_kernel(s_ref, da_ref, h_ref, *, ncc, nch):
    order_f = list(range(nch))
    order_b = list(range(ncc - 1, -1, -1)) + list(range(nch - 1, ncc - 1, -1))
    for d, order in enumerate((order_f, order_b)):
        h = jnp.zeros((B_STATE, B_D_INNER), F32)
        for c in order:
            h_ref[0, c, d] = h
            h = da_ref[0, c, d, 0:1, :] * h + s_ref[0, c, d]


def _ssd_out_kernel(prev_ref, cur_ref, next_ref, cw_ref, cb_ref, dtb_ref, alog_ref, z_ref, h_ref, dsk_ref,
                    nw_ref, o_ref, *, ncc, nch):
    c = pl.program_id(1)
    xs, bm, cm, cum, dtx, cumx, aend = _ssd_prep(prev_ref, cur_ref, next_ref, cw_ref, cb_ref, dtb_ref,
                                                 alog_ref, c, ncc, nch)
    half = B_D_INNER // 2
    lane_s = _lane_iota((CH, LANES))
    lane_h = _lane_iota((B_STATE, B_D_INNER))
    lane_y = _lane_iota((CH, half)) // B_HEADDIM
    cmb = cm.astype(BF16)
    bmb = bm.astype(BF16)
    zero = jnp.zeros_like(cmb)
    gmat = [_dot_nt(jnp.where((lane_s // B_STATE) == g, cmb, zero), bmb) for g in range(B_GROUPS)]
    cumt = cum.T
    li = _row_iota((CH, CH))
    si = _lane_iota((CH, CH))
    y = xs * (dsk_ref[0:1, :] + dsk_ref[1:2, :])
    for d in range(2):
        tri = (si <= li) if d == 0 else (si >= li)
        hc = h_ref[0, 0, d]
        h2 = jnp.concatenate([jnp.where(lane_h < half, hc, 0.0), jnp.where(lane_h < half, 0.0, hc)], axis=0)
        y = y + _dot(cmb, h2.astype(BF16)) * jnp.exp(cumx[d])
        xdt = (xs * dtx[d]).astype(BF16)
        parts = []
        for g in range(B_GROUPS):
            acc = jnp.zeros((CH, half), F32)
            for hl in range(B_HEADS // B_GROUPS):
                idx = d * B_HEADS + g * (B_HEADS // B_GROUPS) + hl
                seg = cum[:, idx:idx + 1] - cumt[idx:idx + 1, :]
                dec = jnp.where(tri, jnp.exp(jnp.where(tri, seg, 0.0)), 0.0)
                sc = (gmat[g] * dec).astype(BF16)
                yh = _dot(sc, xdt[:, g * half:(g + 1) * half])
                acc = jnp.where(lane_y == hl, yh, acc)
            parts.append(acc)
        y = y + jnp.concatenate(parts, axis=1)
    gz = y * _silu(z_ref[0])
    outs = []
    for g in range(B_GROUPS):
        gg = gz[:, g * half:(g + 1) * half]
        ms = jnp.mean(gg * gg, axis=-1, keepdims=True)
        outs.append(gg * lax.rsqrt(ms + EPS))
    o_ref[0] = (jnp.concatenate(outs, axis=1) * nw_ref[...]).astype(BF16)


def _ssd(z, xb, conv_w, conv_b, dt_bias, a_log, d_skip, norm_w, lc):
    B, S, _ = z.shape
    nch, ncc = S // CH, lc // CH
    pad = lambda v: jnp.pad(v.reshape(1, -1), ((0, 0), (0, LANES - v.size)))
    consts = [conv_w, conv_b.reshape(1, XBC_W), pad(dt_bias), pad(a_log)]
    specs = _ssd_specs(S, B)
    st_spec = pl.BlockSpec((1, 1, 2, B_STATE, B_D_INNER), lambda b, c: (b, c, 0, 0, 0))
    da_spec = pl.BlockSpec((1, 1, 2, SUBLANES, B_D_INNER), lambda b, c: (b, c, 0, 0, 0))
    st_shape = jax.ShapeDtypeStruct((B, nch, 2, B_STATE, B_D_INNER), F32)
    da_shape = jax.ShapeDtypeStruct((B, nch, 2, SUBLANES, B_D_INNER), F32)
    small = 16 * CH * XB_W * 4
    states, da = pl.pallas_call(
        functools.partial(_ssd_state_kernel, ncc=ncc, nch=nch),
        grid=(B, nch), in_specs=specs, out_specs=[st_spec, da_spec], out_shape=[st_shape, da_shape],
        compiler_params=_cparams(("parallel", "parallel"), small),
        name="ssd_state",
    )(xb, xb, xb, *consts)
    st_bytes = nch * 2 * B_STATE * B_D_INNER * 4
    hstart = pl.pallas_call(
        functools.partial(_ssd_scan_kernel, ncc=ncc, nch=nch),
        grid=(B,),
        in_specs=[pl.BlockSpec((1, nch, 2, B_STATE, B_D_INNER), lambda b: (b, 0, 0, 0, 0)),
                  pl.BlockSpec((1, nch, 2, SUBLANES, B_D_INNER), lambda b: (b, 0, 0, 0, 0))],
        out_specs=pl.BlockSpec((1, nch, 2, B_STATE, B_D_INNER), lambda b: (b, 0, 0, 0, 0)),
        out_shape=st_shape,
        compiler_params=_cparams(("parallel",), 5 * st_bytes // 2),
        name="ssd_scan",
    )(states, da)
    dsk = jnp.repeat(d_skip, B_HEADDIM, axis=1)
    return pl.pallas_call(
        functools.partial(_ssd_out_kernel, ncc=ncc, nch=nch),
        grid=(B, nch),
        in_specs=specs + [pl.BlockSpec((1, CH, B_D_INNER), lambda b, c: (b, c, 0)), st_spec,
                          pl.BlockSpec((2, B_D_INNER), lambda b, c: (0, 0)),
                          pl.BlockSpec((1, B_D_INNER), lambda b, c: (0, 0))],
        out_specs=pl.BlockSpec((1, CH, B_D_INNER), lambda b, c: (b, c, 0)),
        out_shape=jax.ShapeDtypeStruct((B, S, B_D_INNER), BF16),
        compiler_params=_cparams(("parallel", "parallel"), small),
        name="ssd_out",
    )(xb, xb, xb, *consts, z, hstart, dsk, norm_w.reshape(1, B_D_INNER))


def _outproj_kernel(oa_ref, ob_ref, oc_ref, x_ref, mod_ref, n2_ref, w_ref, wr_ref, xo_ref, h2_ref, rt_ref,
                    cnt_ref):
    @pl.when((pl.program_id(0) == 0) & (pl.program_id(1) == 0))
    def _():
        cnt_ref[...] = jnp.zeros_like(cnt_ref)

    a_w = A_HEADS * A_HEAD_DIM
    mix = (_dot(oa_ref[0], w_ref[0:a_w, :]) + _dot(ob_ref[0], w_ref[a_w:a_w + B_D_INNER, :])
           + _dot(oc_ref[0], w_ref[a_w + B_D_INNER:, :]))
    x = x_ref[0] + mod_ref[0, 2:3, :] * mix
    xo_ref[0] = x
    ms = jnp.mean(x * x, axis=-1, keepdims=True)
    h2 = x * lax.rsqrt(ms + EPS) * n2_ref[...]
    h2 = h2 * (1.0 + mod_ref[0, 4:5, :]) + mod_ref[0, 3:4, :]
    h2_ref[0] = h2
    logit = jnp.dot(h2, wr_ref[...], precision=HI, preferred_element_type=F32)
    lf = _lane_iota(logit.shape).astype(F32)
    big = 1e9
    gmask = lf < N_GROUPS
    gl = jnp.where(gmask, logit, NEG)
    gm = jnp.max(gl, axis=-1, keepdims=True)
    g_p = 1.0 / jnp.sum(jnp.exp(gl - gm), axis=-1, keepdims=True)
    g_sel = jnp.min(jnp.where(gmask & (gl == gm), lf, big), axis=-1, keepdims=True)
    lo = N_GROUPS + EXPERTS_PER_GROUP * g_sel
    emask = (lf >= lo) & (lf < lo + EXPERTS_PER_GROUP)
    el = jnp.where(emask, logit, NEG)
    v1 = jnp.max(el, axis=-1, keepdims=True)
    i1 = jnp.min(jnp.where(emask & (el == v1), lf, big), axis=-1, keepdims=True)
    rest = emask & (lf != i1)
    el2 = jnp.where(rest, logit, NEG)
    v2 = jnp.max(el2, axis=-1, keepdims=True)
    i2 = jnp.min(jnp.where(rest & (el2 == v2), lf, big), axis=-1, keepdims=True)
    t = jnp.exp(v2 - v1)
    gate1 = g_p / (1.0 + t)
    gate2 = g_p * t / (1.0 + t)
    oh1 = lf == i1 - N_GROUPS
    oh2 = lf == i2 - N_GROUPS
    oh = jnp.where(oh1 | oh2, 1.0, 0.0)
    earlier = (_lane_iota((TM, TM)) < _row_iota((TM, TM))).astype(BF16)
    prefix = _dot(earlier, oh.astype(BF16)) + cnt_ref[0:1, :]
    rank1 = jnp.sum(jnp.where(oh1, prefix, 0.0), axis=-1, keepdims=True)
    rank2 = jnp.sum(jnp.where(oh2, prefix, 0.0), axis=-1, keepdims=True)
    cnt_ref[...] = cnt_ref[...] + jnp.sum(oh, axis=0, keepdims=True)
    info = jnp.zeros_like(logit)
    for k, v in enumerate((i1 - N_GROUPS, i2 - N_GROUPS, gate1, gate2, rank1, rank2)):
        info = jnp.where(lf == k, v, info)
    rt_ref[0] = info


RT_E, RT_GATE, RT_RANK = 0, 2, 4


def _outproj(oa, ob, oc, x, mod_l, n2w, w_out, w_r):
    B, S, D = x.shape
    row = lambda w: pl.BlockSpec((1, TM, w), lambda b, j: (b, j, 0))
    mixw = w_out.shape[0]
    return pl.pallas_call(
        _outproj_kernel,
        grid=(B, S // TM),
        in_specs=[row(oa.shape[2]), row(ob.shape[2]), row(oc.shape[2]), row(D),
                  pl.BlockSpec((1, 6, D), _mod_index(B)), pl.BlockSpec((1, D), lambda b, j: (0, 0)),
                  pl.BlockSpec((mixw, D), lambda b, j: (0, 0)), pl.BlockSpec((D, LANES), lambda b, j: (0, 0))],
        out_specs=[row(D), row(D), row(LANES), pl.BlockSpec((SUBLANES, LANES), lambda b, j: (0, 0))],
        out_shape=[jax.ShapeDtypeStruct((B, S, D), F32), jax.ShapeDtypeStruct((B, S, D), F32),
                   jax.ShapeDtypeStruct((B, S, LANES), F32), jax.ShapeDtypeStruct((SUBLANES, LANES), F32)],
        compiler_params=_cparams(("arbitrary", "arbitrary"), 2 * mixw * D * 2 + 14 * TM * D * 4),
        name="outproj",
    )(oa, ob, oc, x, mod_l, n2w.reshape(1, D), w_out, w_r)


def _expert_kernel(be_ref, nu_ref, x_ref, wg_ref, wu_ref, wd_ref, o_ref, wgb_ref, wub_ref, wdb_ref):
    i = pl.program_id(0)
    used = i < nu_ref[0]
    new_expert = (i == 0) | (be_ref[i] != be_ref[jnp.maximum(i - 1, 0)])

    @pl.when(used & new_expert)
    def _():
        wgb_ref[...] = wg_ref[0, 0].astype(BF16)
        wub_ref[...] = wu_ref[0, 0].astype(BF16)
        wdb_ref[...] = wd_ref[0, 0].astype(BF16)

    @pl.when(used)
    def _():
        xb = x_ref[...].astype(BF16)
        hid = _silu(_dot(xb, wgb_ref[...])) * _dot(xb, wub_ref[...])
        o_ref[...] = _dot(hid.astype(BF16), wdb_ref[...])

    @pl.when(jnp.logical_not(used))
    def _():
        o_ref[...] = jnp.zeros_like(o_ref)


def _expert_mlp(buf, block_e, n_used, w_gate, w_up, w_down, layer):
    rows, D = buf.shape
    de = w_gate.shape[3]
    nblk = rows // MOE_ROWS
    wspec = lambda r, c: pl.BlockSpec((1, 1, r, c), lambda i, be, nu: (layer, be[i], 0, 0))
    grid_spec = pltpu.PrefetchScalarGridSpec(
        num_scalar_prefetch=2, grid=(nblk,),
        in_specs=[pl.BlockSpec((MOE_ROWS, D), lambda i, be, nu: (i, 0)), wspec(D, de), wspec(D, de), wspec(de, D)],
        out_specs=pl.BlockSpec((MOE_ROWS, D), lambda i, be, nu: (i, 0)),
        scratch_shapes=[pltpu.VMEM((D, de), BF16), pltpu.VMEM((D, de), BF16), pltpu.VMEM((de, D), BF16)])
    return pl.pallas_call(
        _expert_kernel, grid_spec=grid_spec,
        out_shape=jax.ShapeDtypeStruct((rows, D), F32),
        compiler_params=_cparams(("arbitrary",), 2 * 3 * D * de * 4 + 3 * D * de * 2 + 8 * MOE_ROWS * D * 4),
        name="expert_mlp",
    )(block_e, n_used, buf, w_gate, w_up, w_down)


def _dispatch_plan(route, counts):
    T = route.shape[0]
    n_blocks = -(-T * TOP_K // MOE_ROWS) + N_EXPERTS
    cnt = counts[0, :N_EXPERTS].astype(jnp.int32)
    nb = (cnt + MOE_ROWS - 1) // MOE_ROWS
    blk_end = jnp.cumsum(nb)
    row_start = (blk_end - nb) * MOE_ROWS
    e = route[:, RT_E:RT_E + TOP_K].astype(jnp.int32)
    rank = route[:, RT_RANK:RT_RANK + TOP_K].astype(jnp.int32)
    table = jnp.where(e[:, :, None] == jnp.arange(N_EXPERTS)[None, None, :], row_start[None, None, :], 0)
    pos = (jnp.sum(table, axis=-1) + rank).reshape(T // TM, 1, TM * TOP_K)
    block_e = jnp.sum(blk_end[None, :] <= jnp.arange(n_blocks)[:, None], axis=1)
    block_e = jnp.minimum(block_e, N_EXPERTS - 1).astype(jnp.int32)
    n_used = blk_end[-1].astype(jnp.int32).reshape(1)
    return pos.astype(jnp.int32), block_e, n_used, n_blocks


def _row_copy(src, i, dst, j, sem):
    return pltpu.make_async_copy(src.at[pl.ds(i, 1)], dst.at[pl.ds(j, 1)], sem)


def _dispatch_kernel(pos_ref, x_ref, init_ref, buf_ref, sem):
    del init_ref

    def issue(i, carry):
        for k in range(TOP_K):
            _row_copy(x_ref, i, buf_ref, pos_ref[0, 0, TOP_K * i + k], sem).start()
        return carry

    def drain(i, carry):
        for k in range(TOP_K):
            _row_copy(x_ref, i, buf_ref, pos_ref[0, 0, TOP_K * i + k], sem).wait()
        return carry

    lax.fori_loop(0, TM, issue, 0, unroll=DMA_UNROLL)
    lax.fori_loop(0, TM, drain, 0, unroll=DMA_UNROLL)


def _dispatch(h2, pos, n_blocks):
    T, D = h2.shape
    init = jnp.zeros((n_blocks * MOE_ROWS, D), F32)
    return pl.pallas_call(
        _dispatch_kernel,
        grid=(T // TM,),
        in_specs=[pl.BlockSpec((1, 1, TM * TOP_K), lambda i: (i, 0, 0), memory_space=pltpu.SMEM),
                  pl.BlockSpec((TM, D), lambda i: (i, 0)),
                  pl.BlockSpec(memory_space=pl.ANY)],
        out_specs=pl.BlockSpec(memory_space=pl.ANY),
        out_shape=jax.ShapeDtypeStruct(init.shape, F32),
        scratch_shapes=[pltpu.SemaphoreType.DMA],
        input_output_aliases={2: 0},
        compiler_params=_cparams(("arbitrary",), 4 * TM * D * 4),
        name="moe_dispatch",
    )(pos, h2, init)


def _combine_kernel(pos_ref, rt_ref, eo_ref, y_ref, rows_ref, sem):
    def issue(i, carry):
        for k in range(TOP_K):
            _row_copy(eo_ref, pos_ref[0, 0, TOP_K * i + k], rows_ref.at[k], i, sem).start()
        return carry

    def drain(i, carry):
        for k in range(TOP_K):
            _row_copy(eo_ref, pos_ref[0, 0, TOP_K * i + k], rows_ref.at[k], i, sem).wait()
        return carry

    lax.fori_loop(0, TM, issue, 0, unroll=DMA_UNROLL)
    lax.fori_loop(0, TM, drain, 0, unroll=DMA_UNROLL)
    rt = rt_ref[...]
    y_ref[...] = (rt[:, RT_GATE:RT_GATE + 1] * rows_ref[0] + rt[:, RT_GATE + 1:RT_GATE + 2] * rows_ref[1])


def _combine(eo, pos, route):
    T = route.shape[0]
    D = eo.shape[1]
    return pl.pallas_call(
        _combine_kernel,
        grid=(T // TM,),
        in_specs=[pl.BlockSpec((1, 1, TM * TOP_K), lambda i: (i, 0, 0), memory_space=pltpu.SMEM),
                  pl.BlockSpec((TM, LANES), lambda i: (i, 0)),
                  pl.BlockSpec(memory_space=pl.ANY)],
        out_specs=pl.BlockSpec((TM, D), lambda i: (i, 0)),
        out_shape=jax.ShapeDtypeStruct((T, D), F32),
        scratch_shapes=[pltpu.VMEM((TOP_K, TM, D), F32), pltpu.SemaphoreType.DMA],
        compiler_params=_cparams(("arbitrary",), 6 * TM * D * 4),
        name="moe_combine",
    )(pos, route, eo)


def _final_kernel(x_ref, y_ref, modp_ref, nw_ref, o_ref):
    x = x_ref[0] + modp_ref[0, 5:6, :] * y_ref[0]
    ms = jnp.mean(x * x, axis=-1, keepdims=True)
    o_ref[0] = x * lax.rsqrt(ms + EPS) * nw_ref[...]


def _final(x, y, mod_prev, nw, lc):
    B, S, D = x.shape
    off = lc // TM
    row = lambda w: pl.BlockSpec((1, TM, w), lambda b, j: (b, j + off, 0))
    return pl.pallas_call(
        _final_kernel,
        grid=(B, (S - lc) // TM),
        in_specs=[row(D), row(D), pl.BlockSpec((1, 6, D), lambda b, j: (b, 0, 0)),
                  pl.BlockSpec((1, D), lambda b, j: (0, 0))],
        out_specs=pl.BlockSpec((1, TM, D), lambda b, j: (b, j, 0)),
        out_shape=jax.ShapeDtypeStruct((B, S - lc, D), F32),
        compiler_params=_cparams(("parallel", "parallel"), 8 * TM * D * 4),
        name="final_norm",
    )(x, y, mod_prev, nw.reshape(1, D))


def _pack_w_in(w):
    d = w.shape[0]
    aq, ak, av = w[:, 0:256], w[:, 256:384], w[:, 384:512]
    dup = lambda m: jnp.concatenate([m[:, 0:64], m[:, 0:64], m[:, 64:128], m[:, 64:128]], axis=1)
    bz, bx, bdt = w[:, 512:1024], w[:, 1024:1792], w[:, 1792:1808]
    cq, ck, cv = w[:, 1808:2064], w[:, 2064:2320], w[:, 2320:2576]
    packed = jnp.concatenate([aq, dup(ak), dup(av), bz, bx, bdt, jnp.zeros((d, LANES - 16), w.dtype),
                              cq, ck, cv], axis=1)
    return packed.astype(BF16)


def kernel(x, c, ctx, c_ctx, w_mod, b_mod, norm1_w, norm2_w, w_in, w_out, a_sink, b_conv_w, b_conv_b,
           b_dt_bias, b_a_log, b_d, b_norm_w, c_lambda, c_subln_w, moe_group_router, moe_router,
           moe_w_gate, moe_w_up, moe_w_down, final_norm_w):
    B, L, D = x.shape
    Lc = ctx.shape[1]
    depth = w_mod.shape[0]
    assert Lc == TM and L % TM == 0 and B + 1 <= SUBLANES
    S = Lc + L
    T = B * S
    xa = jnp.concatenate([ctx, x], axis=1)
    c_all = jnp.concatenate([c, c_ctx[None, :], jnp.zeros((SUBLANES - B - 1, D), F32)], axis=0)
    mod = _mod_table(c_all, w_mod, b_mod)
    tabs = _rope_tables(L, Lc, A_HEAD_DIM) + _rope_tables(L, Lc, C_QK_DIM)
    moe, mod_prev = None, None
    for l in range(depth):
        mod_l = mod[l].reshape(SUBLANES, 6, D)
        lambda_init = 0.8 - 0.6 * math.exp(-0.3 * l)
        outs = _inproj(xa, moe, mod_prev, mod_l, norm1_w[l], _pack_w_in(w_in[l]), tabs)
        if moe is not None:
            xa, outs = outs[0], outs[1:]
        qa, ka, va, z, xb, qc, kc, vc = outs
        oa = _attn_a(a_sink[l], qa, ka, va, Lc)
        ob = _ssd(z, xb, b_conv_w[l], b_conv_b[l], b_dt_bias[l], b_a_log[l], b_d[l], b_norm_w[l], Lc)
        oc = _attn_c(qc, kc, vc, c_lambda[l], c_subln_w[l], Lc, lambda_init)
        w_r = jnp.concatenate([moe_group_router[l], moe_router[l],
                               jnp.zeros((D, LANES - N_GROUPS - N_EXPERTS), F32)], axis=1)
        xa, h2, route, counts = _outproj(oa, ob, oc, xa, mod_l, norm2_w[l], w_out[l].astype(BF16), w_r)
        route = route.reshape(T, LANES)
        pos, block_e, n_used, n_blocks = _dispatch_plan(route, counts)
        buf = _dispatch(h2.reshape(T, D), pos, n_blocks)
        eo = _expert_mlp(buf, block_e, n_used, moe_w_gate, moe_w_up, moe_w_down, l)
        moe, mod_prev = _combine(eo, pos, route).reshape(B, S, D), mod_l
    return _final(xa, moe, mod_prev, final_norm_w, Lc)
```

```python
import functools
import math

import jax
import jax.numpy as jnp
from jax import lax
from jax.experimental import pallas as pl
from jax.experimental.pallas import tpu as pltpu

F32 = jnp.float32
BF16 = jnp.bfloat16
HI = lax.Precision.HIGHEST

GRID_W = 64
EPS = 1e-6
ROPE_BASE = 10000.0

A_HEADS = 4
A_KV_HEADS = 2
A_HEAD_DIM = 64
A_WINDOW = 128
B_D_INNER = 512
B_HEADDIM = 64
B_HEADS = B_D_INNER // B_HEADDIM
B_GROUPS = 2
B_STATE = 64
B_CONV = 5
C_HEADS = 4
C_QK_DIM = 32
C_V_DIM = 64
N_GROUPS = 4
EXPERTS_PER_GROUP = 8
N_EXPERTS = N_GROUPS * EXPERTS_PER_GROUP
TOP_K = 2

LANES = 128
SUBLANES = 8
TM = 256
CH = 128
MOE_ROWS = 512
NEG = -1e30
VMEM_CAP = 64 * 1024 * 1024

P_AQ = 0
P_AK = 256
P_AV = 512
P_BZ = 768
P_BX = 1280
P_BDT = 2048
P_CQ = 2176
P_CK = 2432
P_CV = 2688
P_COLS = 2944


def _cparams(sem, vmem_bytes):
    limit = int(min(max(2 * vmem_bytes, 16 * 1024 * 1024), VMEM_CAP - 8 * 1024 * 1024))
    return pltpu.CompilerParams(dimension_semantics=sem, vmem_limit_bytes=limit)


def _lane_iota(shape):
    return lax.broadcasted_iota(jnp.int32, shape, len(shape) - 1)


def _row_iota(shape):
    return lax.broadcasted_iota(jnp.int32, shape, len(shape) - 2)


def _silu(v):
    return v * (1.0 / (1.0 + jnp.exp(-v)))


def _dot(a, b):
    return jnp.dot(a, b, preferred_element_type=F32)


def _dot_nt(a, b):
    return lax.dot_general(a, b, (((1,), (1,)), ((), ())), preferred_element_type=F32)


def _mod_kernel(c_ref, w_ref, b_ref, o_ref):
    s = _silu(c_ref[...]).astype(BF16)
    o_ref[0] = _dot(s, w_ref[0].astype(BF16)) + b_ref[0]


def _mod_table(c_all, w_mod, b_mod):
    depth, d, d6 = w_mod.shape
    tn = 1536
    return pl.pallas_call(
        _mod_kernel,
        grid=(depth, d6 // tn),
        in_specs=[
            pl.BlockSpec((SUBLANES, d), lambda l, n: (0, 0)),
            pl.BlockSpec((1, d, tn), lambda l, n: (l, 0, n)),
            pl.BlockSpec((1, 1, tn), lambda l, n: (l, 0, n)),
        ],
        out_specs=pl.BlockSpec((1, SUBLANES, tn), lambda l, n: (l, 0, n)),
        out_shape=jax.ShapeDtypeStruct((depth, SUBLANES, d6), F32),
        compiler_params=_cparams(("parallel", "parallel"), 2 * d * tn * 4 + d * tn * 2),
        name="mod_table",
    )(c_all, w_mod, b_mod.reshape(depth, 1, d6))


def _rope_tables(L, Lc, dim):
    half = dim // 2
    nf = half // 2
    rows = L // GRID_W
    row_pos = jnp.repeat(jnp.arange(rows, dtype=jnp.int32), GRID_W).astype(F32)
    col_pos = jnp.tile(jnp.arange(GRID_W, dtype=jnp.int32), rows).astype(F32)
    freqs = ROPE_BASE ** (-jnp.arange(nf, dtype=F32) / nf)
    d = jnp.arange(dim)
    within = d % half
    f = freqs[within % nf]
    first = within < nf
    pos = jnp.where((d // half == 0)[None, :], row_pos[:, None], col_pos[:, None])
    ang = pos * f[None, :]
    cos = jnp.cos(ang)
    sin = jnp.where(first[None, :], -jnp.sin(ang), jnp.sin(ang))
    cos = jnp.concatenate([jnp.ones((Lc, dim), F32), cos], axis=0)
    sin = jnp.concatenate([jnp.zeros((Lc, dim), F32), sin], axis=0)
    reps = LANES // dim
    return jnp.tile(cos, (1, reps)), jnp.tile(sin, (1, reps))


def _rope(t, cos, sin, nf):
    w = t.shape[1]
    reps = w // LANES
    cosw = jnp.concatenate([cos] * reps, axis=1) if reps > 1 else cos
    sinw = jnp.concatenate([sin] * reps, axis=1) if reps > 1 else sin
    lane = _lane_iota(t.shape)
    first = (lane & (2 * nf - 1)) < nf
    partner = jnp.where(first, pltpu.roll(t, w - nf, axis=1), pltpu.roll(t, nf, axis=1))
    return t * cosw + partner * sinw


def _row_copy(src, i, dst, j, sem):
    return pltpu.make_async_copy(src.at[pl.ds(i, 1)], dst.at[pl.ds(j, 1)], sem)


def _gather_rows(eo_ref, pos_ref, dst_ref, sem, op):
    for r in range(TM):
        for k in range(TOP_K):
            cp = _row_copy(eo_ref, pos_ref[0, 0, TOP_K * r + k], dst_ref.at[k], r, sem)
            cp.start() if op == "start" else cp.wait()


def _inproj_kernel(*refs, has_moe):
    if not has_moe:
        x_ref, rest = refs[0], refs[1:]
        _inproj_compute(x_ref[0], *rest)
        return
    (pos_ref, posn_ref, x_ref, rt_ref, eo_ref, modp_ref, mod_ref, n1_ref, w_ref, cosa_ref, sina_ref,
     cosc_ref, sinc_ref, xo_ref, qa_ref, ka_ref, va_ref, z_ref, xb_ref, qc_ref, kc_ref, vc_ref,
     rows_ref, sem) = refs
    slot = pl.program_id(0) % 2

    @pl.when(pl.program_id(0) == 0)
    def _():
        _gather_rows(eo_ref, pos_ref, rows_ref.at[0], sem.at[0], "start")
        _gather_rows(eo_ref, pos_ref, rows_ref.at[0], sem.at[0], "wait")

    _gather_rows(eo_ref, posn_ref, rows_ref.at[1 - slot], sem.at[1 - slot], "start")
    rt = rt_ref[0]
    y = rt[:, RT_GATE:RT_GATE + 1] * rows_ref[slot, 0] + rt[:, RT_GATE + 1:RT_GATE + 2] * rows_ref[slot, 1]
    x = x_ref[0] + modp_ref[0, 5:6, :] * y
    xo_ref[0] = x
    _inproj_compute(x, mod_ref, n1_ref, w_ref, cosa_ref, sina_ref, cosc_ref, sinc_ref,
                    qa_ref, ka_ref, va_ref, z_ref, xb_ref, qc_ref, kc_ref, vc_ref)
    _gather_rows(eo_ref, posn_ref, rows_ref.at[1 - slot], sem.at[1 - slot], "wait")


def _inproj_compute(x, mod_ref, n1_ref, w_ref, cosa_ref, sina_ref, cosc_ref, sinc_ref,
                    qa_ref, ka_ref, va_ref, z_ref, xb_ref, qc_ref, kc_ref, vc_ref):
    ms = jnp.mean(x * x, axis=-1, keepdims=True)
    h = x * lax.rsqrt(ms + EPS) * n1_ref[...]
    h = h * (1.0 + mod_ref[0, 1:2, :]) + mod_ref[0, 0:1, :]
    hb = h.astype(BF16)

    def proj(lo, hi):
        return _dot(hb, w_ref[:, lo:hi])

    cosa, sina = cosa_ref[...], sina_ref[...]
    cosc, sinc = cosc_ref[...], sinc_ref[...]
    nfa = A_HEAD_DIM // 4
    nfc = C_QK_DIM // 4
    qa_ref[0] = (_rope(proj(P_AQ, P_AK), cosa, sina, nfa) * (A_HEAD_DIM ** -0.5)).astype(BF16)
    ka_ref[0] = _rope(proj(P_AK, P_AV), cosa, sina, nfa).astype(BF16)
    va_ref[0] = proj(P_AV, P_BZ).astype(BF16)
    z_ref[0] = proj(P_BZ, P_BX)
    xb_ref[0] = proj(P_BX, P_CQ)
    qc_ref[0] = (_rope(proj(P_CQ, P_CK), cosc, sinc, nfc) * (C_QK_DIM ** -0.5)).astype(BF16)
    kc_ref[0] = _rope(proj(P_CK, P_CV), cosc, sinc, nfc).astype(BF16)
    vc_ref[0] = proj(P_CV, P_COLS).astype(BF16)


def _mod_index(nb):
    return lambda b, j: (jnp.where(j == 0, nb, b), 0, 0)


def _inproj(x, moe, mod_prev, mod_l, n1w, w_p, tabs):
    B, S, D = x.shape
    nblk = S // TM
    nsteps = B * nblk
    row = lambda w: pl.BlockSpec((1, TM, w), lambda i: (i // nblk, i % nblk, 0))
    tab = pl.BlockSpec((TM, LANES), lambda i: (i % nblk, 0))
    modspec = pl.BlockSpec((1, 6, D), lambda i: (jnp.where(i % nblk == 0, B, i // nblk), 0, 0))
    const = lambda r, c: pl.BlockSpec((r, c), lambda i: (0, 0))
    has_moe = moe is not None
    in_specs, args, scratch = [], [], []
    if has_moe:
        eo, pos, route = moe
        posspec = lambda f: pl.BlockSpec((1, 1, TM * TOP_K), lambda i: (f(i), 0, 0), memory_space=pltpu.SMEM)
        in_specs += [posspec(lambda i: i), posspec(lambda i: jnp.minimum(i + 1, nsteps - 1)), row(D),
                     row(LANES), pl.BlockSpec(memory_space=pl.ANY), modspec]
        args += [pos, pos, x, route.reshape(B, S, LANES), eo, mod_prev]
        scratch = [pltpu.VMEM((2, TOP_K, TM, D), F32), pltpu.SemaphoreType.DMA((2,))]
    else:
        in_specs += [row(D)]
        args += [x]
    in_specs += [modspec, const(1, D), const(D, P_COLS), tab, tab, tab, tab]
    args += [mod_l, n1w.reshape(1, D), w_p, *tabs]
    widths = [(256, BF16), (256, BF16), (256, BF16), (512, F32), (896, F32), (256, BF16), (256, BF16),
              (256, BF16)]
    out_specs = [row(w) for w, _ in widths]
    out_shape = [jax.ShapeDtypeStruct((B, S, w), dt) for w, dt in widths]
    if has_moe:
        out_specs = [row(D)] + out_specs
        out_shape = [jax.ShapeDtypeStruct((B, S, D), F32)] + out_shape
    vmem = 2 * D * P_COLS * 2 + 12 * TM * D * 4 + 4 * TM * P_COLS * 4
    return pl.pallas_call(
        functools.partial(_inproj_kernel, has_moe=has_moe),
        grid=(nsteps,), in_specs=in_specs, out_specs=out_specs, out_shape=out_shape, scratch_shapes=scratch,
        compiler_params=_cparams(("arbitrary",), vmem),
        name="inproj",
    )(*args)


def _attn_a_kernel(sink_ref, q_ref, k_ref, v_ref, o_ref, *, lc):
    j = pl.program_id(1)
    S = k_ref.shape[1]
    span = TM + 2 * A_WINDOW
    start = jnp.clip(j * TM - A_WINDOW, 0, S - span)
    start = pl.multiple_of(start, LANES)
    qrow = j * TM + _row_iota((TM, span))
    krow = start + _lane_iota((TM, span))
    mask = (jnp.abs(krow - qrow) <= A_WINDOW) & (krow >= lc) & (j > 0)
    mask2 = jnp.concatenate([mask, mask], axis=0)
    lane = _lane_iota((TM, LANES))
    low = lane < A_HEAD_DIM
    top = _row_iota((2 * TM, 1)) < TM
    for c in range(A_KV_HEADS):
        cs = slice(c * LANES, (c + 1) * LANES)
        q = q_ref[0, :, cs]
        zero = jnp.zeros_like(q)
        qs = jnp.concatenate([jnp.where(low, q, zero), jnp.where(low, zero, q)], axis=0)
        kl = k_ref[0, pl.ds(start, span), cs]
        vl = v_ref[0, pl.ds(start, span), cs]
        kc = k_ref[0, 0:lc, cs]
        vc = v_ref[0, 0:lc, cs]
        s_l = jnp.where(mask2, _dot_nt(qs, kl), NEG)
        s_c = _dot_nt(qs, kc)
        sink = jnp.where(top, sink_ref[2 * c], sink_ref[2 * c + 1])
        m = jnp.maximum(jnp.maximum(jnp.max(s_l, axis=-1, keepdims=True),
                                    jnp.max(s_c, axis=-1, keepdims=True)), sink)
        p_l = jnp.exp(s_l - m)
        p_c = jnp.exp(s_c - m)
        den = (jnp.sum(p_l, axis=-1, keepdims=True) + jnp.sum(p_c, axis=-1, keepdims=True)
               + jnp.exp(sink - m))
        o = (_dot(p_l.astype(BF16), vl) + _dot(p_c.astype(BF16), vc)) * (1.0 / den)
        o_ref[0, :, cs] = jnp.where(low, o[:TM], o[TM:]).astype(BF16)


def _attn_a(sink, qa, ka, va, lc):
    B, S, W = qa.shape
    nblk = S // TM
    blk = pl.BlockSpec((1, TM, W), lambda b, j: (b, j, 0))
    full = pl.BlockSpec((1, S, W), lambda b, j: (b, 0, 0))
    return pl.pallas_call(
        functools.partial(_attn_a_kernel, lc=lc),
        grid=(B, nblk),
        in_specs=[pl.BlockSpec(memory_space=pltpu.SMEM), blk, full, full],
        out_specs=blk,
        out_shape=jax.ShapeDtypeStruct((B, S, W), BF16),
        compiler_params=_cparams(("parallel", "parallel"), 4 * S * W * 2 + 12 * 2 * TM * 768 * 4),
        name="attn_a",
    )(sink, qa, ka, va)


def _split3(x):
    hi = x.astype(BF16)
    r = x - hi.astype(F32)
    mid = r.astype(BF16)
    lo = (r - mid.astype(F32)).astype(BF16)
    return hi, mid, lo


def _dot_f32_by_01(x, e01):
    return _dot(jnp.concatenate(_split3(x), axis=1), jnp.concatenate([e01] * 3, axis=0))


def _dot_01_by_f32(t01, x):
    n = x.shape[1]
    r = _dot(t01, jnp.concatenate(_split3(x), axis=1))
    return r[:, :n] + r[:, n:2 * n] + r[:, 2 * n:]


KEY_CHUNK = 512


def _attn_c_body(q_ref, k_ref, v_ref, cl_ref, sw_ref, o_ref, s_ref, chunks, lambda_init):
    cl = cl_ref[...]
    lam = (jnp.exp(jnp.sum(cl[0:1] * cl[1:2], axis=-1, keepdims=True))
           - jnp.exp(jnp.sum(cl[2:3] * cl[3:4], axis=-1, keepdims=True)) + lambda_init)
    lane = _lane_iota((TM, LANES))
    seg_r = _row_iota((LANES, LANES)) // C_V_DIM
    seg_c = _lane_iota((LANES, LANES)) // C_V_DIM
    headsum = (seg_r == seg_c).astype(BF16)
    lane2 = _lane_iota((2 * TM, LANES))
    mrun, m, acc, qs, outs = None, None, None, None, []
    for t in range(C_HEADS + 1):
        if t >= 1:
            php, phh, pslot = (t - 1) // 2, (t - 1) % 2, (t - 1) % 2
            pcs = slice(php * LANES, (php + 1) * LANES)
            m = jnp.max(mrun, axis=-1, keepdims=True)
            acc = jnp.zeros((2 * TM, LANES), F32)
        if t < C_HEADS:
            hp, hh, slot = t // 2, t % 2, t % 2
            cs = slice(hp * LANES, (hp + 1) * LANES)
            q = q_ref[0, :, cs]
            zero = jnp.zeros_like(q)
            base = hh * 2 * C_QK_DIM
            in0 = (lane >= base) & (lane < base + C_QK_DIM)
            in1 = (lane >= base + C_QK_DIM) & (lane < base + 2 * C_QK_DIM)
            qs = jnp.concatenate([jnp.where(in0, q, zero), jnp.where(in1, q, zero)], axis=0)
            mrun = jnp.full((2 * TM, LANES), NEG, F32)
        for st, sz in chunks:
            if t < C_HEADS:
                s = _dot_nt(qs, k_ref[0, st:st + sz, cs])
                s_ref[slot, :, st:st + sz] = s
                for u in range(sz // LANES):
                    mrun = jnp.maximum(mrun, s[:, u * LANES:(u + 1) * LANES])
            if t >= 1:
                e = jnp.exp(s_ref[pslot, :, st:st + sz] - m).astype(BF16)
                vs = v_ref[0, st:st + sz, pcs]
                own = (_lane_iota(vs.shape) // C_V_DIM) == phh
                acc = acc + _dot(e, jnp.where(own, vs, jnp.ones_like(vs)))
        if t >= 1:
            own_o = (lane2 // C_V_DIM) == phh
            on = acc / jnp.where(own_o, pltpu.roll(acc, C_V_DIM, axis=1), 1.0)
            outs.append(on[:TM] - lam * on[TM:])
            if phh == 1:
                o = jnp.where(lane < C_V_DIM, outs[0], outs[1])
                ss = _dot_f32_by_01(o * o, headsum)
                y = o * lax.rsqrt(ss * (1.0 / C_V_DIM) + EPS) * sw_ref[...] * (1.0 - lambda_init)
                o_ref[0, :, pcs] = y.astype(BF16)
                outs = []


def _attn_c_kernel(q_ref, k_ref, v_ref, cl_ref, sw_ref, o_ref, s_ref, *, lc, lambda_init):
    j = pl.program_id(1)
    S = k_ref.shape[1]
    ctx_chunks = [(0, lc)]
    all_chunks = ctx_chunks + [(st, KEY_CHUNK) for st in range(lc, S, KEY_CHUNK)]

    @pl.when(j == 0)
    def _():
        _attn_c_body(q_ref, k_ref, v_ref, cl_ref, sw_ref, o_ref, s_ref, ctx_chunks, lambda_init)

    @pl.when(j > 0)
    def _():
        _attn_c_body(q_ref, k_ref, v_ref, cl_ref, sw_ref, o_ref, s_ref, all_chunks, lambda_init)


def _attn_c(qc, kc, vc, c_lambda, subln_w, lc, lambda_init):
    B, S, W = qc.shape
    assert lc == TM and (S - lc) % KEY_CHUNK == 0
    blk = pl.BlockSpec((1, TM, W), lambda b, j: (b, j, 0))
    full = pl.BlockSpec((1, S, W), lambda b, j: (b, 0, 0))
    sw = jnp.tile(subln_w, LANES // C_V_DIM).reshape(1, LANES)
    return pl.pallas_call(
        functools.partial(_attn_c_kernel, lc=lc, lambda_init=lambda_init),
        grid=(B, S // TM),
        in_specs=[blk, full, full, pl.BlockSpec((4, C_QK_DIM), lambda b, j: (0, 0)),
                  pl.BlockSpec((1, LANES), lambda b, j: (0, 0))],
        out_specs=blk,
        out_shape=jax.ShapeDtypeStruct((B, S, W), BF16),
        scratch_shapes=[pltpu.VMEM((2, 2 * TM, S), F32)],
        compiler_params=_cparams(("parallel", "parallel"),
                                 4 * S * W * 2 + 2 * 2 * TM * S * 4 + 6 * 2 * TM * KEY_CHUNK * 4),
        name="attn_c",
    )(qc, kc, vc, c_lambda, sw)


XBC_W = B_D_INNER + 2 * B_GROUPS * B_STATE
XB_W = XBC_W + LANES


def _ssd_prep(prev_ref, cur_ref, next_ref, cw_ref, cb_ref, dtb_ref, alog_ref, c, ncc, nch):
    cur = cur_ref[0]
    first = (c == 0) | (c == ncc)
    last = (c == ncc - 1) | (c == nch - 1)
    prev = jnp.where(first, 0.0, prev_ref[0][:, :XBC_W])
    nxt = jnp.where(last, 0.0, next_ref[0][:, :XBC_W])
    ext = jnp.concatenate([prev, cur[:, :XBC_W], nxt], axis=0)
    rows = CH + 2 * SUBLANES
    acc = jnp.zeros((CH, XBC_W), F32) + cb_ref[...]
    for k in range(B_CONV):
        sh = (B_CONV // 2 - k) % rows
        r = ext if sh == 0 else pltpu.roll(ext, sh, axis=0)
        acc = acc + r[SUBLANES:SUBLANES + CH] * cw_ref[k:k + 1, :]
    u = _silu(acc)
    xs = u[:, :B_D_INNER]
    bm = u[:, B_D_INNER:B_D_INNER + LANES]
    cm = u[:, B_D_INNER + LANES:]
    xdt_raw = cur[:, XBC_W:] + dtb_ref[...]
    dt = jnp.maximum(xdt_raw, 0.0) + jnp.log(1.0 + jnp.exp(-jnp.abs(xdt_raw)))
    dta = dt * (-jnp.exp(alog_ref[...]))
    li = _row_iota((CH, CH))
    si = _lane_iota((CH, CH))
    tl = (si <= li).astype(BF16)
    tu = (si >= li).astype(BF16)
    lane = _lane_iota((CH, LANES))
    cum = jnp.where(lane < B_HEADS, _dot_01_by_f32(tl, dta), _dot_01_by_f32(tu, dta))
    er = _row_iota((LANES, B_D_INNER))
    ec = _lane_iota((LANES, B_D_INNER)) // B_HEADDIM
    dtx, cumx = [], []
    for d in range(2):
        e = (er == ec + d * B_HEADS).astype(BF16)
        dtx.append(_dot_f32_by_01(dt, e))
        cumx.append(_dot_f32_by_01(cum, e))
    aend = [cumx[0][CH - 1:CH, :], cumx[1][0:1, :]]
    return xs, bm, cm, cum, dtx, cumx, aend


def _ssd_specs(S, B):
    n8 = S // SUBLANES
    per = CH // SUBLANES
    prev = pl.BlockSpec((1, SUBLANES, XB_W), lambda b, c: (b, jnp.maximum(c * per - 1, 0), 0))
    cur = pl.BlockSpec((1, CH, XB_W), lambda b, c: (b, c, 0))
    nxt = pl.BlockSpec((1, SUBLANES, XB_W), lambda b, c: (b, jnp.minimum(c * per + per, n8 - 1), 0))
    const = lambda r, w: pl.BlockSpec((r, w), lambda b, c: (0, 0))
    return [prev, cur, nxt, const(B_CONV, XBC_W), const(1, XBC_W), const(1, LANES), const(1, LANES)]


def _ssd_state_kernel(prev_ref, cur_ref, next_ref, cw_ref, cb_ref, dtb_ref, alog_ref, s_ref, da_ref,
                      *, ncc, nch):
    c = pl.program_id(1)
    xs, bm, cm, cum, dtx, cumx, aend = _ssd_prep(prev_ref, cur_ref, next_ref, cw_ref, cb_ref, dtb_ref,
                                                 alog_ref, c, ncc, nch)
    bmt = bm.T.astype(BF16)
    lane = _lane_iota((B_STATE, B_D_INNER))
    for d in range(2):
        xdec = (xs * dtx[d] * jnp.exp(aend[d] - cumx[d])).astype(BF16)
        s2 = _dot(bmt, xdec)
        s_ref[0, 0, d] = jnp.where(lane < B_D_INNER // 2, s2[:B_STATE], s2[B_STATE:])
        da_ref[0, 0, d] = jnp.broadcast_to(jnp.exp(aend[d]), (SUBLANES, B_D_INNER))


def _ssd_scan_kernel(s_ref, da_ref, h_ref, *, ncc, nch):
    order_f = list(range(nch))
    order_b = list(range(ncc - 1, -1, -1)) + list(range(nch - 1, ncc - 1, -1))
    for d, order in enumerate((order_f, order_b)):
        h = jnp.zeros((B_STATE, B_D_INNER), F32)
        for c in order:
            h_ref[0, c, d] = h
            h = da_ref[0, c, d, 0:1, :] * h + s_ref[0, c, d]


def _ssd_out_kernel(prev_ref, cur_ref, next_ref, cw_ref, cb_ref, dtb_ref, alog_ref, z_ref, h_ref, dsk_ref,
                    nw_ref, o_ref, *, ncc, nch):
    c = pl.program_id(1)
    xs, bm, cm, cum, dtx, cumx, aend = _ssd_prep(prev_ref, cur_ref, next_ref, cw_ref, cb_ref, dtb_ref,
                                                 alog_ref, c, ncc, nch)
    half = B_D_INNER // 2
    lane_s = _lane_iota((CH, LANES))
    lane_h = _lane_iota((B_STATE, B_D_INNER))
    lane_y = _lane_iota((CH, half)) // B_HEADDIM
    cmb = cm.astype(BF16)
    bmb = bm.astype(BF16)
    zero = jnp.zeros_like(cmb)
    gmat = [_dot_nt(jnp.where((lane_s // B_STATE) == g, cmb, zero), bmb) for g in range(B_GROUPS)]
    cumt = cum.T
    li = _row_iota((CH, CH))
    si = _lane_iota((CH, CH))
    y = xs * (dsk_ref[0:1, :] + dsk_ref[1:2, :])
    for d in range(2):
        tri = (si <= li) if d == 0 else (si >= li)
        hc = h_ref[0, 0, d]
        h2 = jnp.concatenate([jnp.where(lane_h < half, hc, 0.0), jnp.where(lane_h < half, 0.0, hc)], axis=0)
        y = y + _dot(cmb, h2.astype(BF16)) * jnp.exp(cumx[d])
        xdt = (xs * dtx[d]).astype(BF16)
        parts = []
        for g in range(B_GROUPS):
            acc = jnp.zeros((CH, half), F32)
            for hl in range(B_HEADS // B_GROUPS):
                idx = d * B_HEADS + g * (B_HEADS // B_GROUPS) + hl
                seg = cum[:, idx:idx + 1] - cumt[idx:idx + 1, :]
                dec = jnp.where(tri, jnp.exp(jnp.where(tri, seg, 0.0)), 0.0)
                sc = (gmat[g] * dec).astype(BF16)
                yh = _dot(sc, xdt[:, g * half:(g + 1) * half])
                acc = jnp.where(lane_y == hl, yh, acc)
            parts.append(acc)
        y = y + jnp.concatenate(parts, axis=1)
    gz = y * _silu(z_ref[0])
    outs = []
    for g in range(B_GROUPS):
        gg = gz[:, g * half:(g + 1) * half]
        ms = jnp.mean(gg * gg, axis=-1, keepdims=True)
        outs.append(gg * lax.rsqrt(ms + EPS))
    o_ref[0] = (jnp.concatenate(outs, axis=1) * nw_ref[...]).astype(BF16)


def _ssd(z, xb, conv_w, conv_b, dt_bias, a_log, d_skip, norm_w, lc):
    B, S, _ = z.shape
    nch, ncc = S // CH, lc // CH
    pad = lambda v: jnp.pad(v.reshape(1, -1), ((0, 0), (0, LANES - v.size)))
    consts = [conv_w, conv_b.reshape(1, XBC_W), pad(dt_bias), pad(a_log)]
    specs = _ssd_specs(S, B)
    st_spec = pl.BlockSpec((1, 1, 2, B_STATE, B_D_INNER), lambda b, c: (b, c, 0, 0, 0))
    da_spec = pl.BlockSpec((1, 1, 2, SUBLANES, B_D_INNER), lambda b, c: (b, c, 0, 0, 0))
    st_shape = jax.ShapeDtypeStruct((B, nch, 2, B_STATE, B_D_INNER), F32)
    da_shape = jax.ShapeDtypeStruct((B, nch, 2, SUBLANES, B_D_INNER), F32)
    small = 16 * CH * XB_W * 4
    states, da = pl.pallas_call(
        functools.partial(_ssd_state_kernel, ncc=ncc, nch=nch),
        grid=(B, nch), in_specs=specs, out_specs=[st_spec, da_spec], out_shape=[st_shape, da_shape],
        compiler_params=_cparams(("parallel", "parallel"), small),
        name="ssd_state",
    )(xb, xb, xb, *consts)
    st_bytes = nch * 2 * B_STATE * B_D_INNER * 4
    hstart = pl.pallas_call(
        functools.partial(_ssd_scan_kernel, ncc=ncc, nch=nch),
        grid=(B,),
        in_specs=[pl.BlockSpec((1, nch, 2, B_STATE, B_D_INNER), lambda b: (b, 0, 0, 0, 0)),
                  pl.BlockSpec((1, nch, 2, SUBLANES, B_D_INNER), lambda b: (b, 0, 0, 0, 0))],
        out_specs=pl.BlockSpec((1, nch, 2, B_STATE, B_D_INNER), lambda b: (b, 0, 0, 0, 0)),
        out_shape=st_shape,
        compiler_params=_cparams(("parallel",), 5 * st_bytes // 2),
        name="ssd_scan",
    )(states, da)
    dsk = jnp.repeat(d_skip, B_HEADDIM, axis=1)
    return pl.pallas_call(
        functools.partial(_ssd_out_kernel, ncc=ncc, nch=nch),
        grid=(B, nch),
        in_specs=specs + [pl.BlockSpec((1, CH, B_D_INNER), lambda b, c: (b, c, 0)), st_spec,
                          pl.BlockSpec((2, B_D_INNER), lambda b, c: (0, 0)),
                          pl.BlockSpec((1, B_D_INNER), lambda b, c: (0, 0))],
        out_specs=pl.BlockSpec((1, CH, B_D_INNER), lambda b, c: (b, c, 0)),
        out_shape=jax.ShapeDtypeStruct((B, S, B_D_INNER), BF16),
        compiler_params=_cparams(("parallel", "parallel"), small),
        name="ssd_out",
    )(xb, xb, xb, *consts, z, hstart, dsk, norm_w.reshape(1, B_D_INNER))


def _outproj_kernel(oa_ref, ob_ref, oc_ref, x_ref, mod_ref, n2_ref, w_ref, wr_ref, xo_ref, h2_ref, rt_ref,
                    cnt_ref):
    @pl.when((pl.program_id(0) == 0) & (pl.program_id(1) == 0))
    def _():
        cnt_ref[...] = jnp.zeros_like(cnt_ref)

    a_w = A_HEADS * A_HEAD_DIM
    mix = (_dot(oa_ref[0], w_ref[0:a_w, :]) + _dot(ob_ref[0], w_ref[a_w:a_w + B_D_INNER, :])
           + _dot(oc_ref[0], w_ref[a_w + B_D_INNER:, :]))
    x = x_ref[0] + mod_ref[0, 2:3, :] * mix
    xo_ref[0] = x
    ms = jnp.mean(x * x, axis=-1, keepdims=True)
    h2 = x * lax.rsqrt(ms + EPS) * n2_ref[...]
    h2 = h2 * (1.0 + mod_ref[0, 4:5, :]) + mod_ref[0, 3:4, :]
    h2_ref[0] = h2
    h_hi, h_mid, _ = _split3(h2)
    logit = _dot(jnp.concatenate([h_hi, h_hi, h_mid], axis=1), wr_ref[...])
    lf = _lane_iota(logit.shape).astype(F32)
    big = 1e9
    gmask = lf < N_GROUPS
    gl = jnp.where(gmask, logit, NEG)
    gm = jnp.max(gl, axis=-1, keepdims=True)
    g_p = 1.0 / jnp.sum(jnp.exp(gl - gm), axis=-1, keepdims=True)
    g_sel = jnp.min(jnp.where(gmask & (gl == gm), lf, big), axis=-1, keepdims=True)
    lo = N_GROUPS + EXPERTS_PER_GROUP * g_sel
    emask = (lf >= lo) & (lf < lo + EXPERTS_PER_GROUP)
    el = jnp.where(emask, logit, NEG)
    v1 = jnp.max(el, axis=-1, keepdims=True)
    i1 = jnp.min(jnp.where(emask & (el == v1), lf, big), axis=-1, keepdims=True)
    rest = emask & (lf != i1)
    el2 = jnp.where(rest, logit, NEG)
    v2 = jnp.max(el2, axis=-1, keepdims=True)
    i2 = jnp.min(jnp.where(rest & (el2 == v2), lf, big), axis=-1, keepdims=True)
    t = jnp.exp(v2 - v1)
    gate1 = g_p / (1.0 + t)
    gate2 = g_p * t / (1.0 + t)
    oh1 = lf == i1 - N_GROUPS
    oh2 = lf == i2 - N_GROUPS
    oh = jnp.where(oh1 | oh2, 1.0, 0.0)
    earlier = (_lane_iota((TM, TM)) < _row_iota((TM, TM))).astype(BF16)
    prefix = _dot(earlier, oh.astype(BF16)) + cnt_ref[0:1, :]
    rank1 = jnp.sum(jnp.where(oh1, prefix, 0.0), axis=-1, keepdims=True)
    rank2 = jnp.sum(jnp.where(oh2, prefix, 0.0), axis=-1, keepdims=True)
    cnt_ref[...] = cnt_ref[...] + jnp.sum(oh, axis=0, keepdims=True)
    info = jnp.zeros_like(logit)
    for k, v in enumerate((i1 - N_GROUPS, i2 - N_GROUPS, gate1, gate2, rank1, rank2)):
        info = jnp.where(lf == k, v, info)
    rt_ref[0] = info


RT_E, RT_GATE, RT_RANK = 0, 2, 4


def _outproj(oa, ob, oc, x, mod_l, n2w, w_out, w_r):
    B, S, D = x.shape
    row = lambda w: pl.BlockSpec((1, TM, w), lambda b, j: (b, j, 0))
    mixw = w_out.shape[0]
    return pl.pallas_call(
        _outproj_kernel,
        grid=(B, S // TM),
        in_specs=[row(oa.shape[2]), row(ob.shape[2]), row(oc.shape[2]), row(D),
                  pl.BlockSpec((1, 6, D), _mod_index(B)), pl.BlockSpec((1, D), lambda b, j: (0, 0)),
                  pl.BlockSpec((mixw, D), lambda b, j: (0, 0)),
                  pl.BlockSpec((3 * D, LANES), lambda b, j: (0, 0))],
        out_specs=[row(D), row(D), row(LANES), pl.BlockSpec((SUBLANES, LANES), lambda b, j: (0, 0))],
        out_shape=[jax.ShapeDtypeStruct((B, S, D), F32), jax.ShapeDtypeStruct((B, S, D), F32),
                   jax.ShapeDtypeStruct((B, S, LANES), F32), jax.ShapeDtypeStruct((SUBLANES, LANES), F32)],
        compiler_params=_cparams(("arbitrary", "arbitrary"), 2 * mixw * D * 2 + 14 * TM * D * 4),
        name="outproj",
    )(oa, ob, oc, x, mod_l, n2w.reshape(1, D), w_out, w_r)


def _expert_kernel(be_ref, nu_ref, x_ref, wg_ref, wu_ref, wd_ref, o_ref, wgb_ref, wub_ref, wdb_ref):
    i = pl.program_id(0)
    used = i < nu_ref[0]
    new_expert = (i == 0) | (be_ref[i] != be_ref[jnp.maximum(i - 1, 0)])

    @pl.when(used & new_expert)
    def _():
        wgb_ref[...] = wg_ref[0, 0].astype(BF16)
        wub_ref[...] = wu_ref[0, 0].astype(BF16)
        wdb_ref[...] = wd_ref[0, 0].astype(BF16)

    @pl.when(used)
    def _():
        xb = x_ref[...].astype(BF16)
        hid = _silu(_dot(xb, wgb_ref[...])) * _dot(xb, wub_ref[...])
        o_ref[...] = _dot(hid.astype(BF16), wdb_ref[...])

    @pl.when(jnp.logical_not(used))
    def _():
        o_ref[...] = jnp.zeros_like(o_ref)


def _expert_mlp(buf, block_e, n_used, w_gate, w_up, w_down, layer):
    rows, D = buf.shape
    de = w_gate.shape[3]
    nblk = rows // MOE_ROWS
    wspec = lambda r, c: pl.BlockSpec((1, 1, r, c), lambda i, be, nu: (layer, be[i], 0, 0))
    grid_spec = pltpu.PrefetchScalarGridSpec(
        num_scalar_prefetch=2, grid=(nblk,),
        in_specs=[pl.BlockSpec((MOE_ROWS, D), lambda i, be, nu: (jnp.minimum(i, nu[0] - 1), 0)),
                  wspec(D, de), wspec(D, de), wspec(de, D)],
        out_specs=pl.BlockSpec((MOE_ROWS, D), lambda i, be, nu: (i, 0)),
        scratch_shapes=[pltpu.VMEM((D, de), BF16), pltpu.VMEM((D, de), BF16), pltpu.VMEM((de, D), BF16)])
    return pl.pallas_call(
        _expert_kernel, grid_spec=grid_spec,
        out_shape=jax.ShapeDtypeStruct((rows, D), F32),
        compiler_params=_cparams(("arbitrary",), 2 * 3 * D * de * 4 + 3 * D * de * 2 + 8 * MOE_ROWS * D * 4),
        name="expert_mlp",
    )(block_e, n_used, buf, w_gate, w_up, w_down)


def _dispatch_plan(route, counts):
    T = route.shape[0]
    n_blocks = -(-T * TOP_K // MOE_ROWS) + N_EXPERTS
    cnt = counts[0, :N_EXPERTS].astype(jnp.int32)
    nb = (cnt + MOE_ROWS - 1) // MOE_ROWS
    blk_end = jnp.cumsum(nb)
    row_start = (blk_end - nb) * MOE_ROWS
    e = route[:, RT_E:RT_E + TOP_K].astype(jnp.int32)
    rank = route[:, RT_RANK:RT_RANK + TOP_K].astype(jnp.int32)
    table = jnp.where(e[:, :, None] == jnp.arange(N_EXPERTS)[None, None, :], row_start[None, None, :], 0)
    pos = (jnp.sum(table, axis=-1) + rank).reshape(T // TM, 1, TM * TOP_K)
    block_e = jnp.sum(blk_end[None, :] <= jnp.arange(n_blocks)[:, None], axis=1)
    block_e = jnp.minimum(block_e, N_EXPERTS - 1).astype(jnp.int32)
    n_used = blk_end[-1].astype(jnp.int32).reshape(1)
    tails = jnp.concatenate([row_start + cnt, nb * MOE_ROWS - cnt, n_used]).astype(jnp.int32)
    return pos.astype(jnp.int32), block_e, n_used, n_blocks, tails


def _zero_tails(tails_ref, zero_ref, buf_ref, sem, op):
    for e in range(N_EXPERTS):
        start, length = tails_ref[e], tails_ref[N_EXPERTS + e]
        end = start + length
        bit = MOE_ROWS // 2
        while bit >= SUBLANES:
            done = (length // (2 * bit)) * (2 * bit)

            @pl.when((length & bit) != 0)
            def _(bit=bit, done=done, end=end):
                at = pl.multiple_of(end - done - bit, SUBLANES)
                cp = pltpu.make_async_copy(zero_ref.at[pl.ds(0, bit)], buf_ref.at[pl.ds(at, bit)], sem)
                cp.start() if op == "start" else cp.wait()

            bit //= 2
        for r in range(SUBLANES - 1):
            @pl.when(r < (length & (SUBLANES - 1)))
            def _(r=r, start=start):
                cp = _row_copy(zero_ref, 0, buf_ref, start + r, sem)
                cp.start() if op == "start" else cp.wait()
    half = zero_ref.shape[0]
    for blk in range(buf_ref.shape[0] // MOE_ROWS):
        @pl.when(blk >= tails_ref[2 * N_EXPERTS])
        def _(blk=blk):
            for at in range(blk * MOE_ROWS, (blk + 1) * MOE_ROWS, half):
                cp = pltpu.make_async_copy(zero_ref, buf_ref.at[pl.ds(at, half)], sem)
                cp.start() if op == "start" else cp.wait()


def _dispatch_kernel(tails_ref, pos_ref, x_ref, buf_ref, zero_ref, sem, zsem):
    @pl.when(pl.program_id(0) == 0)
    def _():
        zero_ref[...] = jnp.zeros_like(zero_ref)
        _zero_tails(tails_ref, zero_ref, buf_ref, zsem, "start")
        _zero_tails(tails_ref, zero_ref, buf_ref, zsem, "wait")

    for op in ("start", "wait"):
        for r in range(TM):
            for k in range(TOP_K):
                cp = _row_copy(x_ref, r, buf_ref, pos_ref[0, 0, TOP_K * r + k], sem)
                cp.start() if op == "start" else cp.wait()


def _dispatch(h2, pos, tails, n_blocks):
    T, D = h2.shape
    grid_spec = pltpu.PrefetchScalarGridSpec(
        num_scalar_prefetch=1, grid=(T // TM,),
        in_specs=[pl.BlockSpec((1, 1, TM * TOP_K), lambda i, tl: (i, 0, 0), memory_space=pltpu.SMEM),
                  pl.BlockSpec((TM, D), lambda i, tl: (i, 0))],
        out_specs=pl.BlockSpec(memory_space=pl.ANY),
        scratch_shapes=[pltpu.VMEM((MOE_ROWS // 2, D), F32), pltpu.SemaphoreType.DMA, pltpu.SemaphoreType.DMA])
    return pl.pallas_call(
        _dispatch_kernel, grid_spec=grid_spec,
        out_shape=jax.ShapeDtypeStruct((n_blocks * MOE_ROWS, D), F32),
        compiler_params=_cparams(("arbitrary",), 6 * TM * D * 4),
        name="moe_dispatch",
    )(tails, pos, h2)


def _combine_kernel(pos_ref, rt_ref, eo_ref, y_ref, rows_ref, sem):
    _gather_rows(eo_ref, pos_ref, rows_ref, sem, "start")
    _gather_rows(eo_ref, pos_ref, rows_ref, sem, "wait")
    rt = rt_ref[...]
    y_ref[...] = (rt[:, RT_GATE:RT_GATE + 1] * rows_ref[0] + rt[:, RT_GATE + 1:RT_GATE + 2] * rows_ref[1])


def _combine(eo, pos, route):
    T = route.shape[0]
    D = eo.shape[1]
    return pl.pallas_call(
        _combine_kernel,
        grid=(T // TM,),
        in_specs=[pl.BlockSpec((1, 1, TM * TOP_K), lambda i: (i, 0, 0), memory_space=pltpu.SMEM),
                  pl.BlockSpec((TM, LANES), lambda i: (i, 0)),
                  pl.BlockSpec(memory_space=pl.ANY)],
        out_specs=pl.BlockSpec((TM, D), lambda i: (i, 0)),
        out_shape=jax.ShapeDtypeStruct((T, D), F32),
        scratch_shapes=[pltpu.VMEM((TOP_K, TM, D), F32), pltpu.SemaphoreType.DMA],
        compiler_params=_cparams(("arbitrary",), 6 * TM * D * 4),
        name="moe_combine",
    )(pos, route, eo)


def _final_kernel(x_ref, y_ref, modp_ref, nw_ref, o_ref):
    x = x_ref[0] + modp_ref[0, 5:6, :] * y_ref[0]
    ms = jnp.mean(x * x, axis=-1, keepdims=True)
    o_ref[0] = x * lax.rsqrt(ms + EPS) * nw_ref[...]


def _final(x, y, mod_prev, nw, lc):
    B, S, D = x.shape
    off = lc // TM
    row = lambda w: pl.BlockSpec((1, TM, w), lambda b, j: (b, j + off, 0))
    return pl.pallas_call(
        _final_kernel,
        grid=(B, (S - lc) // TM),
        in_specs=[row(D), row(D), pl.BlockSpec((1, 6, D), lambda b, j: (b, 0, 0)),
                  pl.BlockSpec((1, D), lambda b, j: (0, 0))],
        out_specs=pl.BlockSpec((1, TM, D), lambda b, j: (b, j, 0)),
        out_shape=jax.ShapeDtypeStruct((B, S - lc, D), F32),
        compiler_params=_cparams(("parallel", "parallel"), 8 * TM * D * 4),
        name="final_norm",
    )(x, y, mod_prev, nw.reshape(1, D))


def _pack_w_in(w):
    d = w.shape[0]
    aq, ak, av = w[:, 0:256], w[:, 256:384], w[:, 384:512]
    dup = lambda m: jnp.concatenate([m[:, 0:64], m[:, 0:64], m[:, 64:128], m[:, 64:128]], axis=1)
    bz, bx, bdt = w[:, 512:1024], w[:, 1024:1792], w[:, 1792:1808]
    cq, ck, cv = w[:, 1808:2064], w[:, 2064:2320], w[:, 2320:2576]
    packed = jnp.concatenate([aq, dup(ak), dup(av), bz, bx, bdt, jnp.zeros((d, LANES - 16), w.dtype),
                              cq, ck, cv], axis=1)
    return packed.astype(BF16)


def kernel(x, c, ctx, c_ctx, w_mod, b_mod, norm1_w, norm2_w, w_in, w_out, a_sink, b_conv_w, b_conv_b,
           b_dt_bias, b_a_log, b_d, b_norm_w, c_lambda, c_subln_w, moe_group_router, moe_router,
           moe_w_gate, moe_w_up, moe_w_down, final_norm_w):
    B, L, D = x.shape
    Lc = ctx.shape[1]
    depth = w_mod.shape[0]
    assert Lc == TM and L % TM == 0 and B + 1 <= SUBLANES
    S = Lc + L
    T = B * S
    xa = jnp.concatenate([ctx, x], axis=1)
    c_all = jnp.concatenate([c, c_ctx[None, :], jnp.zeros((SUBLANES - B - 1, D), F32)], axis=0)
    mod = _mod_table(c_all, w_mod, b_mod)
    tabs = _rope_tables(L, Lc, A_HEAD_DIM) + _rope_tables(L, Lc, C_QK_DIM)
    moe, mod_prev = None, None
    for l in range(depth):
        mod_l = mod[l].reshape(SUBLANES, 6, D)
        lambda_init = 0.8 - 0.6 * math.exp(-0.3 * l)
        outs = _inproj(xa, moe, mod_prev, mod_l, norm1_w[l], _pack_w_in(w_in[l]), tabs)
        if moe is not None:
            xa, outs = outs[0], outs[1:]
        qa, ka, va, z, xb, qc, kc, vc = outs
        oa = _attn_a(a_sink[l], qa, ka, va, Lc)
        ob = _ssd(z, xb, b_conv_w[l], b_conv_b[l], b_dt_bias[l], b_a_log[l], b_d[l], b_norm_w[l], Lc)
        oc = _attn_c(qc, kc, vc, c_lambda[l], c_subln_w[l], Lc, lambda_init)
        w_r = jnp.concatenate([moe_group_router[l], moe_router[l],
                               jnp.zeros((D, LANES - N_GROUPS - N_EXPERTS), F32)], axis=1)
        w_hi = w_r.astype(BF16)
        w_mid = (w_r - w_hi.astype(F32)).astype(BF16)
        xa, h2, route, counts = _outproj(oa, ob, oc, xa, mod_l, norm2_w[l], w_out[l].astype(BF16),
                                         jnp.concatenate([w_hi, w_mid, w_hi], axis=0))
        route = route.reshape(T, LANES)
        pos, block_e, n_used, n_blocks, tails = _dispatch_plan(route, counts)
        buf = _dispatch(h2.reshape(T, D), pos, tails, n_blocks)
        eo = _expert_mlp(buf, block_e, n_used, moe_w_gate, moe_w_up, moe_w_down, l)
        moe, mod_prev = (eo, pos, route), mod_l
    y = _combine(*moe).reshape(B, S, D)
    return _final(xa, y, mod_prev, final_norm_w, Lc)
```

```python
import functools
import math

import jax
import jax.numpy as jnp
from jax import lax
from jax.experimental import pallas as pl
from jax.experimental.pallas import tpu as pltpu

F32 = jnp.float32
BF16 = jnp.bfloat16
HI = lax.Precision.HIGHEST

GRID_W = 64
EPS = 1e-6
ROPE_BASE = 10000.0

A_HEADS = 4
A_KV_HEADS = 2
A_HEAD_DIM = 64
A_WINDOW = 128
B_D_INNER = 512
B_HEADDIM = 64
B_HEADS = B_D_INNER // B_HEADDIM
B_GROUPS = 2
B_STATE = 64
B_CONV = 5
C_HEADS = 4
C_QK_DIM = 32
C_V_DIM = 64
N_GROUPS = 4
EXPERTS_PER_GROUP = 8
N_EXPERTS = N_GROUPS * EXPERTS_PER_GROUP
TOP_K = 2

LANES = 128
SUBLANES = 8
TM = 256
CH = 128
MOE_ROWS = 512
NEG = -1e30
VMEM_CAP = 64 * 1024 * 1024

P_AQ = 0
P_AK = 256
P_AV = 512
P_BZ = 768
P_BX = 1280
P_BDT = 2048
P_CQ = 2176
P_CK = 2432
P_CV = 2688
P_COLS = 2944


def _cparams(sem, vmem_bytes):
    limit = int(min(max(2 * vmem_bytes, 16 * 1024 * 1024), VMEM_CAP - 8 * 1024 * 1024))
    return pltpu.CompilerParams(dimension_semantics=sem, vmem_limit_bytes=limit)


def _lane_iota(shape):
    return lax.broadcasted_iota(jnp.int32, shape, len(shape) - 1)


def _row_iota(shape):
    return lax.broadcasted_iota(jnp.int32, shape, len(shape) - 2)


def _silu(v):
    return v * (1.0 / (1.0 + jnp.exp(-v)))


def _dot(a, b):
    return jnp.dot(a, b, preferred_element_type=F32)


def _dot_nt(a, b):
    return lax.dot_general(a, b, (((1,), (1,)), ((), ())), preferred_element_type=F32)


def _mod_kernel(c_ref, w_ref, b_ref, o_ref):
    s = _silu(c_ref[...]).astype(BF16)
    o_ref[0] = _dot(s, w_ref[0].astype(BF16)) + b_ref[0]


def _mod_table(c_all, w_mod, b_mod):
    depth, d, d6 = w_mod.shape
    tn = 1536
    return pl.pallas_call(
        _mod_kernel,
        grid=(depth, d6 // tn),
        in_specs=[
            pl.BlockSpec((SUBLANES, d), lambda l, n: (0, 0)),
            pl.BlockSpec((1, d, tn), lambda l, n: (l, 0, n)),
            pl.BlockSpec((1, 1, tn), lambda l, n: (l, 0, n)),
        ],
        out_specs=pl.BlockSpec((1, SUBLANES, tn), lambda l, n: (l, 0, n)),
        out_shape=jax.ShapeDtypeStruct((depth, SUBLANES, d6), F32),
        compiler_params=_cparams(("parallel", "parallel"), 2 * d * tn * 4 + d * tn * 2),
        name="mod_table",
    )(c_all, w_mod, b_mod.reshape(depth, 1, d6))


def _rope_tables(L, Lc, dim):
    half = dim // 2
    nf = half // 2
    rows = L // GRID_W
    row_pos = jnp.repeat(jnp.arange(rows, dtype=jnp.int32), GRID_W).astype(F32)
    col_pos = jnp.tile(jnp.arange(GRID_W, dtype=jnp.int32), rows).astype(F32)
    freqs = ROPE_BASE ** (-jnp.arange(nf, dtype=F32) / nf)
    d = jnp.arange(dim)
    within = d % half
    f = freqs[within % nf]
    first = within < nf
    pos = jnp.where((d // half == 0)[None, :], row_pos[:, None], col_pos[:, None])
    ang = pos * f[None, :]
    cos = jnp.cos(ang)
    sin = jnp.where(first[None, :], -jnp.sin(ang), jnp.sin(ang))
    cos = jnp.concatenate([jnp.ones((Lc, dim), F32), cos], axis=0)
    sin = jnp.concatenate([jnp.zeros((Lc, dim), F32), sin], axis=0)
    reps = LANES // dim
    return jnp.tile(cos, (1, reps)), jnp.tile(sin, (1, reps))


def _rope(t, cos, sin, nf):
    w = t.shape[1]
    reps = w // LANES
    cosw = jnp.concatenate([cos] * reps, axis=1) if reps > 1 else cos
    sinw = jnp.concatenate([sin] * reps, axis=1) if reps > 1 else sin
    lane = _lane_iota(t.shape)
    first = (lane & (2 * nf - 1)) < nf
    partner = jnp.where(first, pltpu.roll(t, w - nf, axis=1), pltpu.roll(t, nf, axis=1))
    return t * cosw + partner * sinw


def _row_copy(src, i, dst, j, sem):
    return pltpu.make_async_copy(src.at[pl.ds(i, 1)], dst.at[pl.ds(j, 1)], sem)


def _gather_rows(eo_ref, pos_ref, dst_ref, sem, op):
    for r in range(TM):
        for k in range(TOP_K):
            cp = _row_copy(eo_ref, pos_ref[0, 0, TOP_K * r + k], dst_ref.at[k], r, sem)
            cp.start() if op == "start" else cp.wait()


def _inproj_kernel(*refs, has_moe):
    if not has_moe:
        x_ref, rest = refs[0], refs[1:]
        _inproj_compute(x_ref[0], *rest)
        return
    (pos_ref, posn_ref, x_ref, rt_ref, eo_ref, modp_ref, mod_ref, n1_ref, w_ref, cosa_ref, sina_ref,
     cosc_ref, sinc_ref, xo_ref, qa_ref, ka_ref, va_ref, z_ref, xb_ref, qc_ref, kc_ref, vc_ref,
     rows_ref, sem) = refs
    slot = pl.program_id(0) % 2

    @pl.when(pl.program_id(0) == 0)
    def _():
        _gather_rows(eo_ref, pos_ref, rows_ref.at[0], sem.at[0], "start")
        _gather_rows(eo_ref, pos_ref, rows_ref.at[0], sem.at[0], "wait")

    _gather_rows(eo_ref, posn_ref, rows_ref.at[1 - slot], sem.at[1 - slot], "start")
    rt = rt_ref[0]
    y = rt[:, RT_GATE:RT_GATE + 1] * rows_ref[slot, 0] + rt[:, RT_GATE + 1:RT_GATE + 2] * rows_ref[slot, 1]
    x = x_ref[0] + modp_ref[0, 5:6, :] * y
    xo_ref[0] = x
    _inproj_compute(x, mod_ref, n1_ref, w_ref, cosa_ref, sina_ref, cosc_ref, sinc_ref,
                    qa_ref, ka_ref, va_ref, z_ref, xb_ref, qc_ref, kc_ref, vc_ref)
    _gather_rows(eo_ref, posn_ref, rows_ref.at[1 - slot], sem.at[1 - slot], "wait")


def _inproj_compute(x, mod_ref, n1_ref, w_ref, cosa_ref, sina_ref, cosc_ref, sinc_ref,
                    qa_ref, ka_ref, va_ref, z_ref, xb_ref, qc_ref, kc_ref, vc_ref):
    ms = jnp.mean(x * x, axis=-1, keepdims=True)
    h = x * lax.rsqrt(ms + EPS) * n1_ref[...]
    h = h * (1.0 + mod_ref[0, 1:2, :]) + mod_ref[0, 0:1, :]
    hb = h.astype(BF16)

    def proj(lo, hi):
        return _dot(hb, w_ref[:, lo:hi])

    cosa, sina = cosa_ref[...], sina_ref[...]
    cosc, sinc = cosc_ref[...], sinc_ref[...]
    nfa = A_HEAD_DIM // 4
    nfc = C_QK_DIM // 4
    qa_ref[0] = (_rope(proj(P_AQ, P_AK), cosa, sina, nfa) * (A_HEAD_DIM ** -0.5)).astype(BF16)
    ka_ref[0] = _rope(proj(P_AK, P_AV), cosa, sina, nfa).astype(BF16)
    va_ref[0] = proj(P_AV, P_BZ).astype(BF16)
    z_ref[0] = proj(P_BZ, P_BX)
    xb_ref[0] = proj(P_BX, P_CQ)
    qc_ref[0] = (_rope(proj(P_CQ, P_CK), cosc, sinc, nfc) * (C_QK_DIM ** -0.5)).astype(BF16)
    kc_ref[0] = _rope(proj(P_CK, P_CV), cosc, sinc, nfc).astype(BF16)
    vc_ref[0] = proj(P_CV, P_COLS).astype(BF16)


def _mod_index(nb):
    return lambda b, j: (jnp.where(j == 0, nb, b), 0, 0)


def _inproj(x, moe, mod_prev, mod_l, n1w, w_p, tabs):
    B, S, D = x.shape
    nblk = S // TM
    nsteps = B * nblk
    row = lambda w: pl.BlockSpec((1, TM, w), lambda i: (i // nblk, i % nblk, 0))
    tab = pl.BlockSpec((TM, LANES), lambda i: (i % nblk, 0))
    modspec = pl.BlockSpec((1, 6, D), lambda i: (jnp.where(i % nblk == 0, B, i // nblk), 0, 0))
    const = lambda r, c: pl.BlockSpec((r, c), lambda i: (0, 0))
    has_moe = moe is not None
    in_specs, args, scratch = [], [], []
    if has_moe:
        eo, pos, route = moe
        posspec = lambda f: pl.BlockSpec((1, 1, TM * TOP_K), lambda i: (f(i), 0, 0), memory_space=pltpu.SMEM)
        in_specs += [posspec(lambda i: i), posspec(lambda i: jnp.minimum(i + 1, nsteps - 1)), row(D),
                     row(LANES), pl.BlockSpec(memory_space=pl.ANY), modspec]
        args += [pos, pos, x, route.reshape(B, S, LANES), eo, mod_prev]
        scratch = [pltpu.VMEM((2, TOP_K, TM, D), F32), pltpu.SemaphoreType.DMA((2,))]
    else:
        in_specs += [row(D)]
        args += [x]
    in_specs += [modspec, const(1, D), const(D, P_COLS), tab, tab, tab, tab]
    args += [mod_l, n1w.reshape(1, D), w_p, *tabs]
    widths = [(256, BF16), (256, BF16), (256, BF16), (512, F32), (896, F32), (256, BF16), (256, BF16),
              (256, BF16)]
    out_specs = [row(w) for w, _ in widths]
    out_shape = [jax.ShapeDtypeStruct((B, S, w), dt) for w, dt in widths]
    if has_moe:
        out_specs = [row(D)] + out_specs
        out_shape = [jax.ShapeDtypeStruct((B, S, D), F32)] + out_shape
    vmem = 2 * D * P_COLS * 2 + 12 * TM * D * 4 + 4 * TM * P_COLS * 4
    return pl.pallas_call(
        functools.partial(_inproj_kernel, has_moe=has_moe),
        grid=(nsteps,), in_specs=in_specs, out_specs=out_specs, out_shape=out_shape, scratch_shapes=scratch,
        compiler_params=_cparams(("arbitrary",), vmem),
        name="inproj",
    )(*args)


def _attn_a_kernel(sink_ref, q_ref, k_ref, v_ref, o_ref, *, lc):
    j = pl.program_id(1)
    S = k_ref.shape[1]
    span = TM + 2 * A_WINDOW
    start = jnp.clip(j * TM - A_WINDOW, 0, S - span)
    start = pl.multiple_of(start, LANES)
    qrow = j * TM + _row_iota((TM, span))
    krow = start + _lane_iota((TM, span))
    mask = (jnp.abs(krow - qrow) <= A_WINDOW) & (krow >= lc) & (j > 0)
    mask2 = jnp.concatenate([mask, mask], axis=0)
    lane = _lane_iota((TM, LANES))
    low = lane < A_HEAD_DIM
    top = _row_iota((2 * TM, 1)) < TM
    for c in range(A_KV_HEADS):
        cs = slice(c * LANES, (c + 1) * LANES)
        q = q_ref[0, :, cs]
        zero = jnp.zeros_like(q)
        qs = jnp.concatenate([jnp.where(low, q, zero), jnp.where(low, zero, q)], axis=0)
        kl = k_ref[0, pl.ds(start, span), cs]
        vl = v_ref[0, pl.ds(start, span), cs]
        kc = k_ref[0, 0:lc, cs]
        vc = v_ref[0, 0:lc, cs]
        s_l = jnp.where(mask2, _dot_nt(qs, kl), NEG)
        s_c = _dot_nt(qs, kc)
        sink = jnp.where(top, sink_ref[2 * c], sink_ref[2 * c + 1])
        m = jnp.maximum(jnp.maximum(jnp.max(s_l, axis=-1, keepdims=True),
                                    jnp.max(s_c, axis=-1, keepdims=True)), sink)
        p_l = jnp.exp(s_l - m)
        p_c = jnp.exp(s_c - m)
        den = (jnp.sum(p_l, axis=-1, keepdims=True) + jnp.sum(p_c, axis=-1, keepdims=True)
               + jnp.exp(sink - m))
        o = (_dot(p_l.astype(BF16), vl) + _dot(p_c.astype(BF16), vc)) * (1.0 / den)
        o_ref[0, :, cs] = jnp.where(low, o[:TM], o[TM:]).astype(BF16)


def _attn_a(sink, qa, ka, va, lc):
    B, S, W = qa.shape
    nblk = S // TM
    blk = pl.BlockSpec((1, TM, W), lambda b, j: (b, j, 0))
    full = pl.BlockSpec((1, S, W), lambda b, j: (b, 0, 0))
    return pl.pallas_call(
        functools.partial(_attn_a_kernel, lc=lc),
        grid=(B, nblk),
        in_specs=[pl.BlockSpec(memory_space=pltpu.SMEM), blk, full, full],
        out_specs=blk,
        out_shape=jax.ShapeDtypeStruct((B, S, W), BF16),
        compiler_params=_cparams(("parallel", "parallel"), 4 * S * W * 2 + 12 * 2 * TM * 768 * 4),
        name="attn_a",
    )(sink, qa, ka, va)


def _split3(x):
    hi = x.astype(BF16)
    r = x - hi.astype(F32)
    mid = r.astype(BF16)
    lo = (r - mid.astype(F32)).astype(BF16)
    return hi, mid, lo


def _dot_f32_by_01(x, e01):
    return _dot(jnp.concatenate(_split3(x), axis=1), jnp.concatenate([e01] * 3, axis=0))


def _dot_01_by_f32(t01, x):
    n = x.shape[1]
    r = _dot(t01, jnp.concatenate(_split3(x), axis=1))
    return r[:, :n] + r[:, n:2 * n] + r[:, 2 * n:]


KEY_CHUNK = 512


def _attn_c_body(q_ref, k_ref, v_ref, cl_ref, sw_ref, o_ref, s_ref, chunks, lambda_init):
    cl = cl_ref[...]
    lam = (jnp.exp(jnp.sum(cl[0:1] * cl[1:2], axis=-1, keepdims=True))
           - jnp.exp(jnp.sum(cl[2:3] * cl[3:4], axis=-1, keepdims=True)) + lambda_init)
    lane = _lane_iota((TM, LANES))
    seg_r = _row_iota((LANES, LANES)) // C_V_DIM
    seg_c = _lane_iota((LANES, LANES)) // C_V_DIM
    headsum = (seg_r == seg_c).astype(BF16)
    lane2 = _lane_iota((2 * TM, LANES))
    mrun, m, acc, qs, outs = None, None, None, None, []
    for t in range(C_HEADS + 1):
        if t >= 1:
            php, phh, pslot = (t - 1) // 2, (t - 1) % 2, (t - 1) % 2
            pcs = slice(php * LANES, (php + 1) * LANES)
            m = jnp.max(mrun, axis=-1, keepdims=True)
            acc = jnp.zeros((2 * TM, LANES), F32)
        if t < C_HEADS:
            hp, hh, slot = t // 2, t % 2, t % 2
            cs = slice(hp * LANES, (hp + 1) * LANES)
            q = q_ref[0, :, cs]
            zero = jnp.zeros_like(q)
            base = hh * 2 * C_QK_DIM
            in0 = (lane >= base) & (lane < base + C_QK_DIM)
            in1 = (lane >= base + C_QK_DIM) & (lane < base + 2 * C_QK_DIM)
            qs = jnp.concatenate([jnp.where(in0, q, zero), jnp.where(in1, q, zero)], axis=0)
            mrun = jnp.full((2 * TM, LANES), NEG, F32)
        for st, sz in chunks:
            if t < C_HEADS:
                s = _dot_nt(qs, k_ref[0, st:st + sz, cs])
                s_ref[slot, :, st:st + sz] = s
                for u in range(sz // LANES):
                    mrun = jnp.maximum(mrun, s[:, u * LANES:(u + 1) * LANES])
            if t >= 1:
                e = jnp.exp(s_ref[pslot, :, st:st + sz] - m).astype(BF16)
                vs = v_ref[0, st:st + sz, pcs]
                own = (_lane_iota(vs.shape) // C_V_DIM) == phh
                acc = acc + _dot(e, jnp.where(own, vs, jnp.ones_like(vs)))
        if t >= 1:
            own_o = (lane2 // C_V_DIM) == phh
            on = acc / jnp.where(own_o, pltpu.roll(acc, C_V_DIM, axis=1), 1.0)
            outs.append(on[:TM] - lam * on[TM:])
            if phh == 1:
                o = jnp.where(lane < C_V_DIM, outs[0], outs[1])
                ss = _dot_f32_by_01(o * o, headsum)
                y = o * lax.rsqrt(ss * (1.0 / C_V_DIM) + EPS) * sw_ref[...] * (1.0 - lambda_init)
                o_ref[0, :, pcs] = y.astype(BF16)
                outs = []


def _attn_c_kernel(q_ref, k_ref, v_ref, cl_ref, sw_ref, o_ref, s_ref, *, lc, lambda_init):
    j = pl.program_id(1)
    S = k_ref.shape[1]
    ctx_chunks = [(0, lc)]
    all_chunks = ctx_chunks + [(st, KEY_CHUNK) for st in range(lc, S, KEY_CHUNK)]

    @pl.when(j == 0)
    def _():
        _attn_c_body(q_ref, k_ref, v_ref, cl_ref, sw_ref, o_ref, s_ref, ctx_chunks, lambda_init)

    @pl.when(j > 0)
    def _():
        _attn_c_body(q_ref, k_ref, v_ref, cl_ref, sw_ref, o_ref, s_ref, all_chunks, lambda_init)


def _attn_c(qc, kc, vc, c_lambda, subln_w, lc, lambda_init):
    B, S, W = qc.shape
    assert lc == TM and (S - lc) % KEY_CHUNK == 0
    blk = pl.BlockSpec((1, TM, W), lambda b, j: (b, j, 0))
    full = pl.BlockSpec((1, S, W), lambda b, j: (b, 0, 0))
    sw = jnp.tile(subln_w, LANES // C_V_DIM).reshape(1, LANES)
    return pl.pallas_call(
        functools.partial(_attn_c_kernel, lc=lc, lambda_init=lambda_init),
        grid=(B, S // TM),
        in_specs=[blk, full, full, pl.BlockSpec((4, C_QK_DIM), lambda b, j: (0, 0)),
                  pl.BlockSpec((1, LANES), lambda b, j: (0, 0))],
        out_specs=blk,
        out_shape=jax.ShapeDtypeStruct((B, S, W), BF16),
        scratch_shapes=[pltpu.VMEM((2, 2 * TM, S), F32)],
        compiler_params=_cparams(("parallel", "parallel"),
                                 4 * S * W * 2 + 2 * 2 * TM * S * 4 + 6 * 2 * TM * KEY_CHUNK * 4),
        name="attn_c",
    )(qc, kc, vc, c_lambda, sw)


XBC_W = B_D_INNER + 2 * B_GROUPS * B_STATE
XB_W = XBC_W + LANES


def _ssd_prep(prev_ref, cur_ref, next_ref, cw_ref, cb_ref, dtb_ref, alog_ref, c, ncc, nch):
    cur = cur_ref[0]
    first = (c == 0) | (c == ncc)
    last = (c == ncc - 1) | (c == nch - 1)
    prev = jnp.where(first, 0.0, prev_ref[0][:, :XBC_W])
    nxt = jnp.where(last, 0.0, next_ref[0][:, :XBC_W])
    ext = jnp.concatenate([prev, cur[:, :XBC_W], nxt], axis=0)
    rows = CH + 2 * SUBLANES
    acc = jnp.zeros((CH, XBC_W), F32) + cb_ref[...]
    for k in range(B_CONV):
        sh = (B_CONV // 2 - k) % rows
        r = ext if sh == 0 else pltpu.roll(ext, sh, axis=0)
        acc = acc + r[SUBLANES:SUBLANES + CH] * cw_ref[k:k + 1, :]
    u = _silu(acc)
    xs = u[:, :B_D_INNER]
    bm = u[:, B_D_INNER:B_D_INNER + LANES]
    cm = u[:, B_D_INNER + LANES:]
    xdt_raw = cur[:, XBC_W:] + dtb_ref[...]
    dt = jnp.maximum(xdt_raw, 0.0) + jnp.log(1.0 + jnp.exp(-jnp.abs(xdt_raw)))
    dta = dt * (-jnp.exp(alog_ref[...]))
    li = _row_iota((CH, CH))
    si = _lane_iota((CH, CH))
    tl = (si <= li).astype(BF16)
    tu = (si >= li).astype(BF16)
    lane = _lane_iota((CH, LANES))
    cum = jnp.where(lane < B_HEADS, _dot_01_by_f32(tl, dta), _dot_01_by_f32(tu, dta))
    er = _row_iota((LANES, B_D_INNER))
    ec = _lane_iota((LANES, B_D_INNER)) // B_HEADDIM
    dtx, cumx = [], []
    for d in range(2):
        e = (er == ec + d * B_HEADS).astype(BF16)
        dtx.append(_dot_f32_by_01(dt, e))
        cumx.append(_dot_f32_by_01(cum, e))
    aend = [cumx[0][CH - 1:CH, :], cumx[1][0:1, :]]
    return xs, bm, cm, cum, dtx, cumx, aend


def _ssd_specs(S, B):
    n8 = S // SUBLANES
    per = CH // SUBLANES
    prev = pl.BlockSpec((1, SUBLANES, XB_W), lambda b, c: (b, jnp.maximum(c * per - 1, 0), 0))
    cur = pl.BlockSpec((1, CH, XB_W), lambda b, c: (b, c, 0))
    nxt = pl.BlockSpec((1, SUBLANES, XB_W), lambda b, c: (b, jnp.minimum(c * per + per, n8 - 1), 0))
    const = lambda r, w: pl.BlockSpec((r, w), lambda b, c: (0, 0))
    return [prev, cur, nxt, const(B_CONV, XBC_W), const(1, XBC_W), const(1, LANES), const(1, LANES)]


def _ssd_state_kernel(prev_ref, cur_ref, next_ref, cw_ref, cb_ref, dtb_ref, alog_ref, s_ref, da_ref,
                      *, ncc, nch):
    c = pl.program_id(1)
    xs, bm, cm, cum, dtx, cumx, aend = _ssd_prep(prev_ref, cur_ref, next_ref, cw_ref, cb_ref, dtb_ref,
                                                 alog_ref, c, ncc, nch)
    bmt = bm.T.astype(BF16)
    lane = _lane_iota((B_STATE, B_D_INNER))
    for d in range(2):
        xdec = (xs * dtx[d] * jnp.exp(aend[d] - cumx[d])).astype(BF16)
        s2 = _dot(bmt, xdec)
        s_ref[0, 0, d] = jnp.where(lane < B_D_INNER // 2, s2[:B_STATE], s2[B_STATE:])
        da_ref[0, 0, d] = jnp.broadcast_to(jnp.exp(aend[d]), (SUBLANES, B_D_INNER))


def _ssd_scan_kernel(s_ref, da_ref, h_ref, *, ncc, nch):
    order_f = list(range(nch))
    order_b = list(range(ncc - 1, -1, -1)) + list(range(nch - 1, ncc - 1, -1))
    for d, order in enumerate((order_f, order_b)):
        h = jnp.zeros((B_STATE, B_D_INNER), F32)
        for c in order:
            h_ref[0, c, d] = h
            h = da_ref[0, c, d, 0:1, :] * h + s_ref[0, c, d]


def _ssd_out_kernel(prev_ref, cur_ref, next_ref, cw_ref, cb_ref, dtb_ref, alog_ref, z_ref, h_ref, dsk_ref,
                    nw_ref, o_ref, *, ncc, nch):
    c = pl.program_id(1)
    xs, bm, cm, cum, dtx, cumx, aend = _ssd_prep(prev_ref, cur_ref, next_ref, cw_ref, cb_ref, dtb_ref,
                                                 alog_ref, c, ncc, nch)
    half = B_D_INNER // 2
    lane_s = _lane_iota((CH, LANES))
    lane_h = _lane_iota((B_STATE, B_D_INNER))
    lane_y = _lane_iota((CH, half)) // B_HEADDIM
    cmb = cm.astype(BF16)
    bmb = bm.astype(BF16)
    zero = jnp.zeros_like(cmb)
    gmat = [_dot_nt(jnp.where((lane_s // B_STATE) == g, cmb, zero), bmb) for g in range(B_GROUPS)]
    cumt = cum.T
    li = _row_iota((CH, CH))
    si = _lane_iota((CH, CH))
    y = xs * (dsk_ref[0:1, :] + dsk_ref[1:2, :])
    for d in range(2):
        tri = (si <= li) if d == 0 else (si >= li)
        hc = h_ref[0, 0, d]
        h2 = jnp.concatenate([jnp.where(lane_h < half, hc, 0.0), jnp.where(lane_h < half, 0.0, hc)], axis=0)
        y = y + _dot(cmb, h2.astype(BF16)) * jnp.exp(cumx[d])
        xdt = (xs * dtx[d]).astype(BF16)
        parts = []
        for g in range(B_GROUPS):
            acc = jnp.zeros((CH, half), F32)
            for hl in range(B_HEADS // B_GROUPS):
                idx = d * B_HEADS + g * (B_HEADS // B_GROUPS) + hl
                seg = cum[:, idx:idx + 1] - cumt[idx:idx + 1, :]
                dec = jnp.where(tri, jnp.exp(jnp.where(tri, seg, 0.0)), 0.0)
                sc = (gmat[g] * dec).astype(BF16)
                yh = _dot(sc, xdt[:, g * half:(g + 1) * half])
                acc = jnp.where(lane_y == hl, yh, acc)
            parts.append(acc)
        y = y + jnp.concatenate(parts, axis=1)
    gz = y * _silu(z_ref[0])
    outs = []
    for g in range(B_GROUPS):
        gg = gz[:, g * half:(g + 1) * half]
        ms = jnp.mean(gg * gg, axis=-1, keepdims=True)
        outs.append(gg * lax.rsqrt(ms + EPS))
    o_ref[0] = (jnp.concatenate(outs, axis=1) * nw_ref[...]).astype(BF16)


def _ssd(z, xb, conv_w, conv_b, dt_bias, a_log, d_skip, norm_w, lc):
    B, S, _ = z.shape
    nch, ncc = S // CH, lc // CH
    pad = lambda v: jnp.pad(v.reshape(1, -1), ((0, 0), (0, LANES - v.size)))
    consts = [conv_w, conv_b.reshape(1, XBC_W), pad(dt_bias), pad(a_log)]
    specs = _ssd_specs(S, B)
    st_spec = pl.BlockSpec((1, 1, 2, B_STATE, B_D_INNER), lambda b, c: (b, c, 0, 0, 0))
    da_spec = pl.BlockSpec((1, 1, 2, SUBLANES, B_D_INNER), lambda b, c: (b, c, 0, 0, 0))
    st_shape = jax.ShapeDtypeStruct((B, nch, 2, B_STATE, B_D_INNER), F32)
    da_shape = jax.ShapeDtypeStruct((B, nch, 2, SUBLANES, B_D_INNER), F32)
    small = 16 * CH * XB_W * 4
    states, da = pl.pallas_call(
        functools.partial(_ssd_state_kernel, ncc=ncc, nch=nch),
        grid=(B, nch), in_specs=specs, out_specs=[st_spec, da_spec], out_shape=[st_shape, da_shape],
        compiler_params=_cparams(("parallel", "parallel"), small),
        name="ssd_state",
    )(xb, xb, xb, *consts)
    st_bytes = nch * 2 * B_STATE * B_D_INNER * 4
    hstart = pl.pallas_call(
        functools.partial(_ssd_scan_kernel, ncc=ncc, nch=nch),
        grid=(B,),
        in_specs=[pl.BlockSpec((1, nch, 2, B_STATE, B_D_INNER), lambda b: (b, 0, 0, 0, 0)),
                  pl.BlockSpec((1, nch, 2, SUBLANES, B_D_INNER), lambda b: (b, 0, 0, 0, 0))],
        out_specs=pl.BlockSpec((1, nch, 2, B_STATE, B_D_INNER), lambda b: (b, 0, 0, 0, 0)),
        out_shape=st_shape,
        compiler_params=_cparams(("parallel",), 5 * st_bytes // 2),
        name="ssd_scan",
    )(states, da)
    dsk = jnp.repeat(d_skip, B_HEADDIM, axis=1)
    return pl.pallas_call(
        functools.partial(_ssd_out_kernel, ncc=ncc, nch=nch),
        grid=(B, nch),
        in_specs=specs + [pl.BlockSpec((1, CH, B_D_INNER), lambda b, c: (b, c, 0)), st_spec,
                          pl.BlockSpec((2, B_D_INNER), lambda b, c: (0, 0)),
                          pl.BlockSpec((1, B_D_INNER), lambda b, c: (0, 0))],
        out_specs=pl.BlockSpec((1, CH, B_D_INNER), lambda b, c: (b, c, 0)),
        out_shape=jax.ShapeDtypeStruct((B, S, B_D_INNER), BF16),
        compiler_params=_cparams(("parallel", "parallel"), small),
        name="ssd_out",
    )(xb, xb, xb, *consts, z, hstart, dsk, norm_w.reshape(1, B_D_INNER))


def _outproj_kernel(oa_ref, ob_ref, oc_ref, x_ref, mod_ref, n2_ref, w_ref, wr_ref, xo_ref, h2_ref, rt_ref,
                    cnt_ref):
    @pl.when((pl.program_id(0) == 0) & (pl.program_id(1) == 0))
    def _():
        cnt_ref[...] = jnp.zeros_like(cnt_ref)

    a_w = A_HEADS * A_HEAD_DIM
    mix = (_dot(oa_ref[0], w_ref[0:a_w, :]) + _dot(ob_ref[0], w_ref[a_w:a_w + B_D_INNER, :])
           + _dot(oc_ref[0], w_ref[a_w + B_D_INNER:, :]))
    x = x_ref[0] + mod_ref[0, 2:3, :] * mix
    xo_ref[0] = x
    ms = jnp.mean(x * x, axis=-1, keepdims=True)
    h2 = x * lax.rsqrt(ms + EPS) * n2_ref[...]
    h2 = h2 * (1.0 + mod_ref[0, 4:5, :]) + mod_ref[0, 3:4, :]
    h2_ref[0] = h2
    h_hi, h_mid, _ = _split3(h2)
    logit = _dot(jnp.concatenate([h_hi, h_hi, h_mid], axis=1), wr_ref[...])
    lf = _lane_iota(logit.shape).astype(F32)
    big = 1e9
    gmask = lf < N_GROUPS
    gl = jnp.where(gmask, logit, NEG)
    gm = jnp.max(gl, axis=-1, keepdims=True)
    g_p = 1.0 / jnp.sum(jnp.exp(gl - gm), axis=-1, keepdims=True)
    g_sel = jnp.min(jnp.where(gmask & (gl == gm), lf, big), axis=-1, keepdims=True)
    lo = N_GROUPS + EXPERTS_PER_GROUP * g_sel
    emask = (lf >= lo) & (lf < lo + EXPERTS_PER_GROUP)
    el = jnp.where(emask, logit, NEG)
    v1 = jnp.max(el, axis=-1, keepdims=True)
    i1 = jnp.min(jnp.where(emask & (el == v1), lf, big), axis=-1, keepdims=True)
    rest = emask & (lf != i1)
    el2 = jnp.where(rest, logit, NEG)
    v2 = jnp.max(el2, axis=-1, keepdims=True)
    i2 = jnp.min(jnp.where(rest & (el2 == v2), lf, big), axis=-1, keepdims=True)
    t = jnp.exp(v2 - v1)
    gate1 = g_p / (1.0 + t)
    gate2 = g_p * t / (1.0 + t)
    oh1 = lf == i1 - N_GROUPS
    oh2 = lf == i2 - N_GROUPS
    oh = jnp.where(oh1 | oh2, 1.0, 0.0)
    earlier = (_lane_iota((TM, TM)) < _row_iota((TM, TM))).astype(BF16)
    prefix = _dot(earlier, oh.astype(BF16)) + cnt_ref[0:1, :]
    rank1 = jnp.sum(jnp.where(oh1, prefix, 0.0), axis=-1, keepdims=True)
    rank2 = jnp.sum(jnp.where(oh2, prefix, 0.0), axis=-1, keepdims=True)
    cnt_ref[...] = cnt_ref[...] + jnp.sum(oh, axis=0, keepdims=True)
    info = jnp.zeros_like(logit)
    for k, v in enumerate((i1 - N_GROUPS, i2 - N_GROUPS, gate1, gate2, rank1, rank2)):
        info = jnp.where(lf == k, v, info)
    rt_ref[0] = info


RT_E, RT_GATE, RT_RANK = 0, 2, 4


def _outproj(oa, ob, oc, x, mod_l, n2w, w_out, w_r):
    B, S, D = x.shape
    row = lambda w: pl.BlockSpec((1, TM, w), lambda b, j: (b, j, 0))
    mixw = w_out.shape[0]
    return pl.pallas_call(
        _outproj_kernel,
        grid=(B, S // TM),
        in_specs=[row(oa.shape[2]), row(ob.shape[2]), row(oc.shape[2]), row(D),
                  pl.BlockSpec((1, 6, D), _mod_index(B)), pl.BlockSpec((1, D), lambda b, j: (0, 0)),
                  pl.BlockSpec((mixw, D), lambda b, j: (0, 0)),
                  pl.BlockSpec((3 * D, LANES), lambda b, j: (0, 0))],
        out_specs=[row(D), row(D), row(LANES), pl.BlockSpec((SUBLANES, LANES), lambda b, j: (0, 0))],
        out_shape=[jax.ShapeDtypeStruct((B, S, D), F32), jax.ShapeDtypeStruct((B, S, D), F32),
                   jax.ShapeDtypeStruct((B, S, LANES), F32), jax.ShapeDtypeStruct((SUBLANES, LANES), F32)],
        compiler_params=_cparams(("arbitrary", "arbitrary"), 2 * mixw * D * 2 + 14 * TM * D * 4),
        name="outproj",
    )(oa, ob, oc, x, mod_l, n2w.reshape(1, D), w_out, w_r)


def _expert_kernel(be_ref, nu_ref, nxt_ref, slot_ref, x_ref, wg_hbm, wu_hbm, wd_hbm, o_ref,
                   wgf_ref, wuf_ref, wdf_ref, wgb_ref, wub_ref, wdb_ref, sem, *, layer):
    i = pl.program_id(0)
    used = i < nu_ref[0]

    def weight_copies(e, s):
        return [pltpu.make_async_copy(src.at[layer, e], dst.at[s], sem.at[s, n])
                for n, (src, dst) in enumerate(((wg_hbm, wgf_ref), (wu_hbm, wuf_ref), (wd_hbm, wdf_ref)))]

    @pl.when(i == 0)
    def _():
        for cp in weight_copies(be_ref[0], 0):
            cp.start()

    @pl.when(used & (nxt_ref[i] >= -1))
    def _():
        s = slot_ref[i]
        for cp in weight_copies(be_ref[i], s):
            cp.wait()

        @pl.when(nxt_ref[i] >= 0)
        def _():
            for cp in weight_copies(nxt_ref[i], 1 - s):
                cp.start()

        wgb_ref[...] = wgf_ref[s].astype(BF16)
        wub_ref[...] = wuf_ref[s].astype(BF16)
        wdb_ref[...] = wdf_ref[s].astype(BF16)

    @pl.when(used)
    def _():
        xb = x_ref[...].astype(BF16)
        hid = _silu(_dot(xb, wgb_ref[...])) * _dot(xb, wub_ref[...])
        o_ref[...] = _dot(hid.astype(BF16), wdb_ref[...])

    @pl.when(jnp.logical_not(used))
    def _():
        o_ref[...] = jnp.zeros_like(o_ref)


def _expert_mlp(buf, block_e, n_used, w_gate, w_up, w_down, layer):
    rows, D = buf.shape
    de = w_gate.shape[3]
    nblk = rows // MOE_ROWS
    idx = jnp.arange(nblk)
    first = (idx < n_used[0]) & ((idx == 0) | (block_e != jnp.roll(block_e, 1)))
    slot = ((jnp.cumsum(first) - 1) % 2).astype(jnp.int32)
    first_pos = jnp.where(first, idx, nblk)
    after = jnp.concatenate([lax.cummin(first_pos, reverse=True)[1:], jnp.full((1,), nblk)])
    nxt_e = jnp.sum(jnp.where(after[:, None] == idx[None, :], block_e[None, :] + 1, 0), axis=1) - 1
    nxt = jnp.where(first, nxt_e, -2).astype(jnp.int32)
    hbm = pl.BlockSpec(memory_space=pl.ANY)
    grid_spec = pltpu.PrefetchScalarGridSpec(
        num_scalar_prefetch=4, grid=(nblk,),
        in_specs=[pl.BlockSpec((MOE_ROWS, D), lambda i, be, nu, nx, sl: (jnp.minimum(i, nu[0] - 1), 0)),
                  hbm, hbm, hbm],
        out_specs=pl.BlockSpec((MOE_ROWS, D), lambda i, be, nu, nx, sl: (i, 0)),
        scratch_shapes=[pltpu.VMEM((2, D, de), F32), pltpu.VMEM((2, D, de), F32), pltpu.VMEM((2, de, D), F32),
                        pltpu.VMEM((D, de), BF16), pltpu.VMEM((D, de), BF16), pltpu.VMEM((de, D), BF16),
                        pltpu.SemaphoreType.DMA((2, 3))])
    return pl.pallas_call(
        functools.partial(_expert_kernel, layer=layer), grid_spec=grid_spec,
        out_shape=jax.ShapeDtypeStruct((rows, D), F32),
        compiler_params=_cparams(("arbitrary",), 2 * 3 * D * de * 4 + 3 * D * de * 2 + 8 * MOE_ROWS * D * 4),
        name="expert_mlp",
    )(block_e, n_used, nxt, slot, buf, w_gate, w_up, w_down)


def _dispatch_plan(route, counts):
    T = route.shape[0]
    n_blocks = -(-T * TOP_K // MOE_ROWS) + N_EXPERTS
    cnt = counts[0, :N_EXPERTS].astype(jnp.int32)
    nb = (cnt + MOE_ROWS - 1) // MOE_ROWS
    blk_end = jnp.cumsum(nb)
    row_start = (blk_end - nb) * MOE_ROWS
    e = route[:, RT_E:RT_E + TOP_K].astype(jnp.int32)
    rank = route[:, RT_RANK:RT_RANK + TOP_K].astype(jnp.int32)
    table = jnp.where(e[:, :, None] == jnp.arange(N_EXPERTS)[None, None, :], row_start[None, None, :], 0)
    pos = (jnp.sum(table, axis=-1) + rank).reshape(T // TM, 1, TM * TOP_K)
    block_e = jnp.sum(blk_end[None, :] <= jnp.arange(n_blocks)[:, None], axis=1)
    block_e = jnp.minimum(block_e, N_EXPERTS - 1).astype(jnp.int32)
    n_used = blk_end[-1].astype(jnp.int32).reshape(1)
    tails = jnp.concatenate([row_start + cnt, nb * MOE_ROWS - cnt, n_used]).astype(jnp.int32)
    return pos.astype(jnp.int32), block_e, n_used, n_blocks, tails


def _zero_tails(tails_ref, zero_ref, buf_ref, sem, op):
    for e in range(N_EXPERTS):
        start, length = tails_ref[e], tails_ref[N_EXPERTS + e]
        end = start + length
        bit = MOE_ROWS // 2
        while bit >= SUBLANES:
            done = (length // (2 * bit)) * (2 * bit)

            @pl.when((length & bit) != 0)
            def _(bit=bit, done=done, end=end):
                at = pl.multiple_of(end - done - bit, SUBLANES)
                cp = pltpu.make_async_copy(zero_ref.at[pl.ds(0, bit)], buf_ref.at[pl.ds(at, bit)], sem)
                cp.start() if op == "start" else cp.wait()

            bit //= 2
        for r in range(SUBLANES - 1):
            @pl.when(r < (length & (SUBLANES - 1)))
            def _(r=r, start=start):
                cp = _row_copy(zero_ref, 0, buf_ref, start + r, sem)
                cp.start() if op == "start" else cp.wait()
    half = zero_ref.shape[0]
    for blk in range(buf_ref.shape[0] // MOE_ROWS):
        @pl.when(blk >= tails_ref[2 * N_EXPERTS])
        def _(blk=blk):
            for at in range(blk * MOE_ROWS, (blk + 1) * MOE_ROWS, half):
                cp = pltpu.make_async_copy(zero_ref, buf_ref.at[pl.ds(at, half)], sem)
                cp.start() if op == "start" else cp.wait()


def _dispatch_kernel(tails_ref, pos_ref, x_ref, buf_ref, zero_ref, sem, zsem):
    @pl.when(pl.program_id(0) == 0)
    def _():
        zero_ref[...] = jnp.zeros_like(zero_ref)
        _zero_tails(tails_ref, zero_ref, buf_ref, zsem, "start")
        _zero_tails(tails_ref, zero_ref, buf_ref, zsem, "wait")

    for op in ("start", "wait"):
        for r in range(TM):
            for k in range(TOP_K):
                cp = _row_copy(x_ref, r, buf_ref, pos_ref[0, 0, TOP_K * r + k], sem)
                cp.start() if op == "start" else cp.wait()


def _dispatch(h2, pos, tails, n_blocks):
    T, D = h2.shape
    grid_spec = pltpu.PrefetchScalarGridSpec(
        num_scalar_prefetch=1, grid=(T // TM,),
        in_specs=[pl.BlockSpec((1, 1, TM * TOP_K), lambda i, tl: (i, 0, 0), memory_space=pltpu.SMEM),
                  pl.BlockSpec((TM, D), lambda i, tl: (i, 0))],
        out_specs=pl.BlockSpec(memory_space=pl.ANY),
        scratch_shapes=[pltpu.VMEM((MOE_ROWS // 2, D), F32), pltpu.SemaphoreType.DMA, pltpu.SemaphoreType.DMA])
    return pl.pallas_call(
        _dispatch_kernel, grid_spec=grid_spec,
        out_shape=jax.ShapeDtypeStruct((n_blocks * MOE_ROWS, D), F32),
        compiler_params=_cparams(("arbitrary",), 6 * TM * D * 4),
        name="moe_dispatch",
    )(tails, pos, h2)


def _final_kernel(pos_ref, posn_ref, x_ref, rt_ref, eo_ref, modp_ref, nw_ref, o_ref, rows_ref, sem):
    slot = pl.program_id(0) % 2

    @pl.when(pl.program_id(0) == 0)
    def _():
        _gather_rows(eo_ref, pos_ref, rows_ref.at[0], sem.at[0], "start")
        _gather_rows(eo_ref, pos_ref, rows_ref.at[0], sem.at[0], "wait")

    _gather_rows(eo_ref, posn_ref, rows_ref.at[1 - slot], sem.at[1 - slot], "start")
    rt = rt_ref[0]
    y = rt[:, RT_GATE:RT_GATE + 1] * rows_ref[slot, 0] + rt[:, RT_GATE + 1:RT_GATE + 2] * rows_ref[slot, 1]
    x = x_ref[0] + modp_ref[0, 5:6, :] * y
    ms = jnp.mean(x * x, axis=-1, keepdims=True)
    o_ref[0] = x * lax.rsqrt(ms + EPS) * nw_ref[...]
    _gather_rows(eo_ref, posn_ref, rows_ref.at[1 - slot], sem.at[1 - slot], "wait")


def _final(x, moe, mod_prev, nw, lc):
    eo, pos, route = moe
    B, S, D = x.shape
    off, nblk = lc // TM, S // TM
    nlat = nblk - off
    nsteps = B * nlat
    blk = lambda i: (i // nlat) * nblk + i % nlat + off
    row = lambda w: pl.BlockSpec((1, TM, w), lambda i: (i // nlat, i % nlat + off, 0))
    posspec = lambda f: pl.BlockSpec((1, 1, TM * TOP_K), lambda i: (f(i), 0, 0), memory_space=pltpu.SMEM)
    return pl.pallas_call(
        _final_kernel,
        grid=(nsteps,),
        in_specs=[posspec(blk), posspec(lambda i: blk(jnp.minimum(i + 1, nsteps - 1))), row(D), row(LANES),
                  pl.BlockSpec(memory_space=pl.ANY), pl.BlockSpec((1, 6, D), lambda i: (i // nlat, 0, 0)),
                  pl.BlockSpec((1, D), lambda i: (0, 0))],
        out_specs=pl.BlockSpec((1, TM, D), lambda i: (i // nlat, i % nlat, 0)),
        out_shape=jax.ShapeDtypeStruct((B, S - lc, D), F32),
        scratch_shapes=[pltpu.VMEM((2, TOP_K, TM, D), F32), pltpu.SemaphoreType.DMA((2,))],
        compiler_params=_cparams(("arbitrary",), 10 * TM * D * 4),
        name="final_norm",
    )(pos, pos, x, route.reshape(B, S, LANES), eo, mod_prev, nw.reshape(1, D))


def _pack_w_in(w):
    d = w.shape[0]
    aq, ak, av = w[:, 0:256], w[:, 256:384], w[:, 384:512]
    dup = lambda m: jnp.concatenate([m[:, 0:64], m[:, 0:64], m[:, 64:128], m[:, 64:128]], axis=1)
    bz, bx, bdt = w[:, 512:1024], w[:, 1024:1792], w[:, 1792:1808]
    cq, ck, cv = w[:, 1808:2064], w[:, 2064:2320], w[:, 2320:2576]
    packed = jnp.concatenate([aq, dup(ak), dup(av), bz, bx, bdt, jnp.zeros((d, LANES - 16), w.dtype),
                              cq, ck, cv], axis=1)
    return packed.astype(BF16)


def kernel(x, c, ctx, c_ctx, w_mod, b_mod, norm1_w, norm2_w, w_in, w_out, a_sink, b_conv_w, b_conv_b,
           b_dt_bias, b_a_log, b_d, b_norm_w, c_lambda, c_subln_w, moe_group_router, moe_router,
           moe_w_gate, moe_w_up, moe_w_down, final_norm_w):
    B, L, D = x.shape
    Lc = ctx.shape[1]
    depth = w_mod.shape[0]
    assert Lc == TM and L % TM == 0 and B + 1 <= SUBLANES
    S = Lc + L
    T = B * S
    xa = jnp.concatenate([ctx, x], axis=1)
    c_all = jnp.concatenate([c, c_ctx[None, :], jnp.zeros((SUBLANES - B - 1, D), F32)], axis=0)
    mod = _mod_table(c_all, w_mod, b_mod)
    tabs = _rope_tables(L, Lc, A_HEAD_DIM) + _rope_tables(L, Lc, C_QK_DIM)
    moe, mod_prev = None, None
    for l in range(depth):
        mod_l = mod[l].reshape(SUBLANES, 6, D)
        lambda_init = 0.8 - 0.6 * math.exp(-0.3 * l)
        outs = _inproj(xa, moe, mod_prev, mod_l, norm1_w[l], _pack_w_in(w_in[l]), tabs)
        if moe is not None:
            xa, outs = outs[0], outs[1:]
        qa, ka, va, z, xb, qc, kc, vc = outs
        oa = _attn_a(a_sink[l], qa, ka, va, Lc)
        ob = _ssd(z, xb, b_conv_w[l], b_conv_b[l], b_dt_bias[l], b_a_log[l], b_d[l], b_norm_w[l], Lc)
        oc = _attn_c(qc, kc, vc, c_lambda[l], c_subln_w[l], Lc, lambda_init)
        w_r = jnp.concatenate([moe_group_router[l], moe_router[l],
                               jnp.zeros((D, LANES - N_GROUPS - N_EXPERTS), F32)], axis=1)
        w_hi = w_r.astype(BF16)
        w_mid = (w_r - w_hi.astype(F32)).astype(BF16)
        xa, h2, route, counts = _outproj(oa, ob, oc, xa, mod_l, norm2_w[l], w_out[l].astype(BF16),
                                         jnp.concatenate([w_hi, w_mid, w_hi], axis=0))
        route = route.reshape(T, LANES)
        pos, block_e, n_used, n_blocks, tails = _dispatch_plan(route, counts)
        buf = _dispatch(h2.reshape(T, D), pos, tails, n_blocks)
        eo = _expert_mlp(buf, block_e, n_used, moe_w_gate, moe_w_up, moe_w_down, l)
        moe, mod_prev = (eo, pos, route), mod_l
    return _final(xa, moe, mod_prev, final_norm_w, Lc)
```

```python
import functools
import math

import jax
import jax.numpy as jnp
from jax import lax
from jax.experimental import pallas as pl
from jax.experimental.pallas import tpu as pltpu

F32 = jnp.float32
BF16 = jnp.bfloat16

GRID_W = 64
EPS = 1e-6
ROPE_BASE = 10000.0

A_HEADS = 4
A_KV_HEADS = 2
A_HEAD_DIM = 64
A_WINDOW = 128
B_D_INNER = 512
B_HEADDIM = 64
B_HEADS = B_D_INNER // B_HEADDIM
B_GROUPS = 2
B_STATE = 64
B_CONV = 5
C_HEADS = 4
C_QK_DIM = 32
C_V_DIM = 64
N_GROUPS = 4
EXPERTS_PER_GROUP = 8
N_EXPERTS = N_GROUPS * EXPERTS_PER_GROUP
TOP_K = 2

LANES = 128
SUBLANES = 8
TM = 256
CH = 128
MOE_ROWS = 512
NEG = -1e30
VMEM_CAP = 64 * 1024 * 1024

P_AQ = 0
P_AK = 256
P_AV = 512
P_BZ = 768
P_BX = 1280
P_CQ = 2176
P_CK = 2432
P_CV = 2688
P_COLS = 2944


def _cparams(sem, vmem_bytes):
    limit = int(min(max(2 * vmem_bytes, 16 * 1024 * 1024), VMEM_CAP - 8 * 1024 * 1024))
    return pltpu.CompilerParams(dimension_semantics=sem, vmem_limit_bytes=limit)


def _lane_iota(shape):
    return lax.broadcasted_iota(jnp.int32, shape, len(shape) - 1)


def _row_iota(shape):
    return lax.broadcasted_iota(jnp.int32, shape, len(shape) - 2)


def _silu(v):
    return v * (1.0 / (1.0 + jnp.exp(-v)))


def _dot(a, b):
    return jnp.dot(a, b, preferred_element_type=F32)


def _dot_nt(a, b):
    return lax.dot_general(a, b, (((1,), (1,)), ((), ())), preferred_element_type=F32)


def _mod_kernel(c_ref, w_ref, b_ref, o_ref):
    s = _silu(c_ref[...]).astype(BF16)
    o_ref[0] = _dot(s, w_ref[0].astype(BF16)) + b_ref[0]


def _mod_table(c_all, w_mod, b_mod):
    depth, d, d6 = w_mod.shape
    tn = 1536
    return pl.pallas_call(
        _mod_kernel,
        grid=(depth, d6 // tn),
        in_specs=[
            pl.BlockSpec((SUBLANES, d), lambda l, n: (0, 0)),
            pl.BlockSpec((1, d, tn), lambda l, n: (l, 0, n)),
            pl.BlockSpec((1, 1, tn), lambda l, n: (l, 0, n)),
        ],
        out_specs=pl.BlockSpec((1, SUBLANES, tn), lambda l, n: (l, 0, n)),
        out_shape=jax.ShapeDtypeStruct((depth, SUBLANES, d6), F32),
        compiler_params=_cparams(("parallel", "parallel"), 2 * d * tn * 4 + d * tn * 2),
        name="mod_table",
    )(c_all, w_mod, b_mod.reshape(depth, 1, d6))


def _rope_tables(L, Lc, dim):
    half = dim // 2
    nf = half // 2
    rows = L // GRID_W
    row_pos = jnp.repeat(jnp.arange(rows, dtype=jnp.int32), GRID_W).astype(F32)
    col_pos = jnp.tile(jnp.arange(GRID_W, dtype=jnp.int32), rows).astype(F32)
    freqs = ROPE_BASE ** (-jnp.arange(nf, dtype=F32) / nf)
    d = jnp.arange(dim)
    within = d % half
    f = freqs[within % nf]
    first = within < nf
    pos = jnp.where((d // half == 0)[None, :], row_pos[:, None], col_pos[:, None])
    ang = pos * f[None, :]
    cos = jnp.cos(ang)
    sin = jnp.where(first[None, :], -jnp.sin(ang), jnp.sin(ang))
    cos = jnp.concatenate([jnp.ones((Lc, dim), F32), cos], axis=0)
    sin = jnp.concatenate([jnp.zeros((Lc, dim), F32), sin], axis=0)
    reps = LANES // dim
    return jnp.tile(cos, (1, reps)), jnp.tile(sin, (1, reps))


def _rope(t, cos, sin, nf):
    w = t.shape[1]
    reps = w // LANES
    cosw = jnp.concatenate([cos] * reps, axis=1) if reps > 1 else cos
    sinw = jnp.concatenate([sin] * reps, axis=1) if reps > 1 else sin
    lane = _lane_iota(t.shape)
    first = (lane & (2 * nf - 1)) < nf
    partner = jnp.where(first, pltpu.roll(t, w - nf, axis=1), pltpu.roll(t, nf, axis=1))
    return t * cosw + partner * sinw


def _row_copy(src, i, dst, j, sem):
    return pltpu.make_async_copy(src.at[pl.ds(i, 1)], dst.at[pl.ds(j, 1)], sem)


def _gather_rows(eo_ref, pos_ref, dst_ref, sem, op):
    for r in range(TM):
        for k in range(TOP_K):
            cp = _row_copy(eo_ref, pos_ref[0, 0, TOP_K * r + k], dst_ref.at[k], r, sem)
            cp.start() if op == "start" else cp.wait()


def _inproj_kernel(*refs, has_moe):
    if not has_moe:
        x_ref, rest = refs[0], refs[1:]
        _inproj_compute(x_ref[0], *rest)
        return
    (pos_ref, posn_ref, x_ref, rt_ref, eo_ref, modp_ref, mod_ref, n1_ref, w_ref, cosa_ref, sina_ref,
     cosc_ref, sinc_ref, xo_ref, qa_ref, ka_ref, va_ref, z_ref, xb_ref, qc_ref, kc_ref, vc_ref,
     rows_ref, sem) = refs
    slot = pl.program_id(0) % 2

    @pl.when(pl.program_id(0) == 0)
    def _():
        _gather_rows(eo_ref, pos_ref, rows_ref.at[0], sem.at[0], "start")
        _gather_rows(eo_ref, pos_ref, rows_ref.at[0], sem.at[0], "wait")

    _gather_rows(eo_ref, posn_ref, rows_ref.at[1 - slot], sem.at[1 - slot], "start")
    rt = rt_ref[0]
    y = rt[:, RT_GATE:RT_GATE + 1] * rows_ref[slot, 0] + rt[:, RT_GATE + 1:RT_GATE + 2] * rows_ref[slot, 1]
    x = x_ref[0] + modp_ref[0, 5:6, :] * y
    xo_ref[0] = x
    _inproj_compute(x, mod_ref, n1_ref, w_ref, cosa_ref, sina_ref, cosc_ref, sinc_ref,
                    qa_ref, ka_ref, va_ref, z_ref, xb_ref, qc_ref, kc_ref, vc_ref)
    _gather_rows(eo_ref, posn_ref, rows_ref.at[1 - slot], sem.at[1 - slot], "wait")


def _inproj_compute(x, mod_ref, n1_ref, w_ref, cosa_ref, sina_ref, cosc_ref, sinc_ref,
                    qa_ref, ka_ref, va_ref, z_ref, xb_ref, qc_ref, kc_ref, vc_ref):
    ms = jnp.mean(x * x, axis=-1, keepdims=True)
    h = x * lax.rsqrt(ms + EPS) * n1_ref[...]
    h = h * (1.0 + mod_ref[0, 1:2, :]) + mod_ref[0, 0:1, :]
    hb = h.astype(BF16)

    def proj(lo, hi):
        return _dot(hb, w_ref[:, lo:hi])

    cosa, sina = cosa_ref[...], sina_ref[...]
    cosc, sinc = cosc_ref[...], sinc_ref[...]
    nfa = A_HEAD_DIM // 4
    nfc = C_QK_DIM // 4
    qa_ref[0] = (_rope(proj(P_AQ, P_AK), cosa, sina, nfa) * (A_HEAD_DIM ** -0.5)).astype(BF16)
    ka_ref[0] = _rope(proj(P_AK, P_AV), cosa, sina, nfa).astype(BF16)
    va_ref[0] = proj(P_AV, P_BZ).astype(BF16)
    z_ref[0] = proj(P_BZ, P_BX)
    xb_ref[0] = proj(P_BX, P_CQ)
    qc_ref[0] = (_rope(proj(P_CQ, P_CK), cosc, sinc, nfc) * (C_QK_DIM ** -0.5)).astype(BF16)
    kc_ref[0] = _rope(proj(P_CK, P_CV), cosc, sinc, nfc).astype(BF16)
    vc_ref[0] = proj(P_CV, P_COLS).astype(BF16)


def _mod_index(nb):
    return lambda b, j: (jnp.where(j == 0, nb, b), 0, 0)


def _inproj(x, moe, mod_prev, mod_l, n1w, w_p, tabs):
    B, S, D = x.shape
    nblk = S // TM
    nsteps = B * nblk
    row = lambda w: pl.BlockSpec((1, TM, w), lambda i: (i // nblk, i % nblk, 0))
    tab = pl.BlockSpec((TM, LANES), lambda i: (i % nblk, 0))
    modspec = pl.BlockSpec((1, 6, D), lambda i: (jnp.where(i % nblk == 0, B, i // nblk), 0, 0))
    const = lambda r, c: pl.BlockSpec((r, c), lambda i: (0, 0))
    has_moe = moe is not None
    in_specs, args, scratch = [], [], []
    if has_moe:
        eo, pos, route = moe
        posspec = lambda f: pl.BlockSpec((1, 1, TM * TOP_K), lambda i: (f(i), 0, 0), memory_space=pltpu.SMEM)
        in_specs += [posspec(lambda i: i), posspec(lambda i: jnp.minimum(i + 1, nsteps - 1)), row(D),
                     row(LANES), pl.BlockSpec(memory_space=pl.ANY), modspec]
        args += [pos, pos, x, route.reshape(B, S, LANES), eo, mod_prev]
        scratch = [pltpu.VMEM((2, TOP_K, TM, D), F32), pltpu.SemaphoreType.DMA((2,))]
    else:
        in_specs += [row(D)]
        args += [x]
    in_specs += [modspec, const(1, D), const(D, P_COLS), tab, tab, tab, tab]
    args += [mod_l, n1w.reshape(1, D), w_p, *tabs]
    widths = [(256, BF16), (256, BF16), (256, BF16), (512, F32), (896, F32), (256, BF16), (256, BF16),
              (256, BF16)]
    out_specs = [row(w) for w, _ in widths]
    out_shape = [jax.ShapeDtypeStruct((B, S, w), dt) for w, dt in widths]
    if has_moe:
        out_specs = [row(D)] + out_specs
        out_shape = [jax.ShapeDtypeStruct((B, S, D), F32)] + out_shape
    vmem = 2 * D * P_COLS * 2 + 12 * TM * D * 4 + 4 * TM * P_COLS * 4
    return pl.pallas_call(
        functools.partial(_inproj_kernel, has_moe=has_moe),
        grid=(nsteps,), in_specs=in_specs, out_specs=out_specs, out_shape=out_shape, scratch_shapes=scratch,
        compiler_params=_cparams(("arbitrary",), vmem),
        name="inproj",
    )(*args)


def _attn_a_kernel(sink_ref, q_ref, k_ref, v_ref, o_ref, *, lc):
    j = pl.program_id(1)
    S = k_ref.shape[1]
    span = TM + 2 * A_WINDOW
    start = jnp.clip(j * TM - A_WINDOW, 0, S - span)
    start = pl.multiple_of(start, LANES)
    qrow = j * TM + _row_iota((TM, span))
    krow = start + _lane_iota((TM, span))
    mask = (jnp.abs(krow - qrow) <= A_WINDOW) & (krow >= lc) & (j > 0)
    mask2 = jnp.concatenate([mask, mask], axis=0)
    lane = _lane_iota((TM, LANES))
    low = lane < A_HEAD_DIM
    top = _row_iota((2 * TM, 1)) < TM
    for c in range(A_KV_HEADS):
        cs = slice(c * LANES, (c + 1) * LANES)
        q = q_ref[0, :, cs]
        zero = jnp.zeros_like(q)
        qs = jnp.concatenate([jnp.where(low, q, zero), jnp.where(low, zero, q)], axis=0)
        kl = k_ref[0, pl.ds(start, span), cs]
        vl = v_ref[0, pl.ds(start, span), cs]
        kc = k_ref[0, 0:lc, cs]
        vc = v_ref[0, 0:lc, cs]
        s_l = jnp.where(mask2, _dot_nt(qs, kl), NEG)
        s_c = _dot_nt(qs, kc)
        sink = jnp.where(top, sink_ref[2 * c], sink_ref[2 * c + 1])
        m = jnp.maximum(jnp.maximum(jnp.max(s_l, axis=-1, keepdims=True),
                                    jnp.max(s_c, axis=-1, keepdims=True)), sink)
        p_l = jnp.exp(s_l - m)
        p_c = jnp.exp(s_c - m)
        den = (jnp.sum(p_l, axis=-1, keepdims=True) + jnp.sum(p_c, axis=-1, keepdims=True)
               + jnp.exp(sink - m))
        o = (_dot(p_l.astype(BF16), vl) + _dot(p_c.astype(BF16), vc)) * (1.0 / den)
        o_ref[0, :, cs] = jnp.where(low, o[:TM], o[TM:]).astype(BF16)


def _attn_a(sink, qa, ka, va, lc):
    B, S, W = qa.shape
    nblk = S // TM
    blk = pl.BlockSpec((1, TM, W), lambda b, j: (b, j, 0))
    full = pl.BlockSpec((1, S, W), lambda b, j: (b, 0, 0))
    return pl.pallas_call(
        functools.partial(_attn_a_kernel, lc=lc),
        grid=(B, nblk),
        in_specs=[pl.BlockSpec(memory_space=pltpu.SMEM), blk, full, full],
        out_specs=blk,
        out_shape=jax.ShapeDtypeStruct((B, S, W), BF16),
        compiler_params=_cparams(("parallel", "parallel"), 4 * S * W * 2 + 12 * 2 * TM * 768 * 4),
        name="attn_a",
    )(sink, qa, ka, va)


def _split3(x):
    hi = x.astype(BF16)
    r = x - hi.astype(F32)
    mid = r.astype(BF16)
    lo = (r - mid.astype(F32)).astype(BF16)
    return hi, mid, lo


def _dot_f32_by_01(x, e01):
    return _dot(jnp.concatenate(_split3(x), axis=1), jnp.concatenate([e01] * 3, axis=0))


def _dot_01_by_f32(t01, x):
    n = x.shape[1]
    r = _dot(t01, jnp.concatenate(_split3(x), axis=1))
    return r[:, :n] + r[:, n:2 * n] + r[:, 2 * n:]


KEY_CHUNK = 512


def _attn_c_body(q_ref, k_ref, v_ref, cl_ref, sw_ref, o_ref, s_ref, chunks, lambda_init):
    cl = cl_ref[...]
    lam = (jnp.exp(jnp.sum(cl[0:1] * cl[1:2], axis=-1, keepdims=True))
           - jnp.exp(jnp.sum(cl[2:3] * cl[3:4], axis=-1, keepdims=True)) + lambda_init)
    lane = _lane_iota((TM, LANES))
    seg_r = _row_iota((LANES, LANES)) // C_V_DIM
    seg_c = _lane_iota((LANES, LANES)) // C_V_DIM
    headsum = (seg_r == seg_c).astype(BF16)
    lane2 = _lane_iota((2 * TM, LANES))
    mrun, m, acc, qs, outs = None, None, None, None, []
    for t in range(C_HEADS + 1):
        if t >= 1:
            php, phh, pslot = (t - 1) // 2, (t - 1) % 2, (t - 1) % 2
            pcs = slice(php * LANES, (php + 1) * LANES)
            m = jnp.max(mrun, axis=-1, keepdims=True)
            acc = jnp.zeros((2 * TM, LANES), F32)
        if t < C_HEADS:
            hp, hh, slot = t // 2, t % 2, t % 2
            cs = slice(hp * LANES, (hp + 1) * LANES)
            q = q_ref[0, :, cs]
            zero = jnp.zeros_like(q)
            base = hh * 2 * C_QK_DIM
            in0 = (lane >= base) & (lane < base + C_QK_DIM)
            in1 = (lane >= base + C_QK_DIM) & (lane < base + 2 * C_QK_DIM)
            qs = jnp.concatenate([jnp.where(in0, q, zero), jnp.where(in1, q, zero)], axis=0)
            mrun = jnp.full((2 * TM, LANES), NEG, F32)
        for st, sz in chunks:
            if t < C_HEADS:
                s = _dot_nt(qs, k_ref[0, st:st + sz, cs])
                s_ref[slot, :, st:st + sz] = s
                for u in range(sz // LANES):
                    mrun = jnp.maximum(mrun, s[:, u * LANES:(u + 1) * LANES])
            if t >= 1:
                e = jnp.exp(s_ref[pslot, :, st:st + sz] - m).astype(BF16)
                vs = v_ref[0, st:st + sz, pcs]
                own = (_lane_iota(vs.shape) // C_V_DIM) == phh
                acc = acc + _dot(e, jnp.where(own, vs, jnp.ones_like(vs)))
        if t >= 1:
            own_o = (lane2 // C_V_DIM) == phh
            on = acc / jnp.where(own_o, pltpu.roll(acc, C_V_DIM, axis=1), 1.0)
            outs.append(on[:TM] - lam * on[TM:])
            if phh == 1:
                o = jnp.where(lane < C_V_DIM, outs[0], outs[1])
                ss = _dot_f32_by_01(o * o, headsum)
                y = o * lax.rsqrt(ss * (1.0 / C_V_DIM) + EPS) * sw_ref[...] * (1.0 - lambda_init)
                o_ref[0, :, pcs] = y.astype(BF16)
                outs = []


def _attn_c_kernel(q_ref, k_ref, v_ref, cl_ref, sw_ref, o_ref, s_ref, *, lc, lambda_init):
    j = pl.program_id(1)
    S = k_ref.shape[1]
    ctx_chunks = [(0, lc)]
    all_chunks = ctx_chunks + [(st, KEY_CHUNK) for st in range(lc, S, KEY_CHUNK)]

    @pl.when(j == 0)
    def _():
        _attn_c_body(q_ref, k_ref, v_ref, cl_ref, sw_ref, o_ref, s_ref, ctx_chunks, lambda_init)

    @pl.when(j > 0)
    def _():
        _attn_c_body(q_ref, k_ref, v_ref, cl_ref, sw_ref, o_ref, s_ref, all_chunks, lambda_init)


def _attn_c(qc, kc, vc, c_lambda, subln_w, lc, lambda_init):
    B, S, W = qc.shape
    assert lc == TM and (S - lc) % KEY_CHUNK == 0
    blk = pl.BlockSpec((1, TM, W), lambda b, j: (b, j, 0))
    full = pl.BlockSpec((1, S, W), lambda b, j: (b, 0, 0))
    sw = jnp.tile(subln_w, LANES // C_V_DIM).reshape(1, LANES)
    return pl.pallas_call(
        functools.partial(_attn_c_kernel, lc=lc, lambda_init=lambda_init),
        grid=(B, S // TM),
        in_specs=[blk, full, full, pl.BlockSpec((4, C_QK_DIM), lambda b, j: (0, 0)),
                  pl.BlockSpec((1, LANES), lambda b, j: (0, 0))],
        out_specs=blk,
        out_shape=jax.ShapeDtypeStruct((B, S, W), BF16),
        scratch_shapes=[pltpu.VMEM((2, 2 * TM, S), F32)],
        compiler_params=_cparams(("parallel", "parallel"),
                                 4 * S * W * 2 + 2 * 2 * TM * S * 4 + 6 * 2 * TM * KEY_CHUNK * 4),
        name="attn_c",
    )(qc, kc, vc, c_lambda, sw)


XBC_W = B_D_INNER + 2 * B_GROUPS * B_STATE
XB_W = XBC_W + LANES


def _ssd_prep(prev_ref, cur_ref, next_ref, cw_ref, cb_ref, dtb_ref, alog_ref, c, ncc, nch):
    cur = cur_ref[0]
    first = (c == 0) | (c == ncc)
    last = (c == ncc - 1) | (c == nch - 1)
    prev = jnp.where(first, 0.0, prev_ref[0][:, :XBC_W])
    nxt = jnp.where(last, 0.0, next_ref[0][:, :XBC_W])
    ext = jnp.concatenate([prev, cur[:, :XBC_W], nxt], axis=0)
    rows = CH + 2 * SUBLANES
    acc = jnp.zeros((CH, XBC_W), F32) + cb_ref[...]
    for k in range(B_CONV):
        sh = (B_CONV // 2 - k) % rows
        r = ext if sh == 0 else pltpu.roll(ext, sh, axis=0)
        acc = acc + r[SUBLANES:SUBLANES + CH] * cw_ref[k:k + 1, :]
    u = _silu(acc)
    xs = u[:, :B_D_INNER]
    bm = u[:, B_D_INNER:B_D_INNER + LANES]
    cm = u[:, B_D_INNER + LANES:]
    xdt_raw = cur[:, XBC_W:] + dtb_ref[...]
    dt = jnp.maximum(xdt_raw, 0.0) + jnp.log(1.0 + jnp.exp(-jnp.abs(xdt_raw)))
    dta = dt * (-jnp.exp(alog_ref[...]))
    li = _row_iota((CH, CH))
    si = _lane_iota((CH, CH))
    tl = (si <= li).astype(BF16)
    tu = (si >= li).astype(BF16)
    lane = _lane_iota((CH, LANES))
    cum = jnp.where(lane < B_HEADS, _dot_01_by_f32(tl, dta), _dot_01_by_f32(tu, dta))
    er = _row_iota((LANES, B_D_INNER))
    ec = _lane_iota((LANES, B_D_INNER)) // B_HEADDIM
    dtx, cumx = [], []
    for d in range(2):
        e = (er == ec + d * B_HEADS).astype(BF16)
        dtx.append(_dot_f32_by_01(dt, e))
        cumx.append(_dot_f32_by_01(cum, e))
    aend = [cumx[0][CH - 1:CH, :], cumx[1][0:1, :]]
    return xs, bm, cm, cum, dtx, cumx, aend


def _ssd_specs(S, B):
    n8 = S // SUBLANES
    per = CH // SUBLANES
    prev = pl.BlockSpec((1, SUBLANES, XB_W), lambda b, c: (b, jnp.maximum(c * per - 1, 0), 0))
    cur = pl.BlockSpec((1, CH, XB_W), lambda b, c: (b, c, 0))
    nxt = pl.BlockSpec((1, SUBLANES, XB_W), lambda b, c: (b, jnp.minimum(c * per + per, n8 - 1), 0))
    const = lambda r, w: pl.BlockSpec((r, w), lambda b, c: (0, 0))
    return [prev, cur, nxt, const(B_CONV, XBC_W), const(1, XBC_W), const(1, LANES), const(1, LANES)]


def _ssd_state_kernel(prev_ref, cur_ref, next_ref, cw_ref, cb_ref, dtb_ref, alog_ref, s_ref, da_ref,
                      *, ncc, nch):
    c = pl.program_id(1)
    xs, bm, cm, cum, dtx, cumx, aend = _ssd_prep(prev_ref, cur_ref, next_ref, cw_ref, cb_ref, dtb_ref,
                                                 alog_ref, c, ncc, nch)
    bmt = bm.T.astype(BF16)
    lane = _lane_iota((B_STATE, B_D_INNER))
    for d in range(2):
        xdec = (xs * dtx[d] * jnp.exp(aend[d] - cumx[d])).astype(BF16)
        s2 = _dot(bmt, xdec)
        s_ref[0, 0, d] = jnp.where(lane < B_D_INNER // 2, s2[:B_STATE], s2[B_STATE:])
        da_ref[0, 0, d] = jnp.broadcast_to(jnp.exp(aend[d]), (SUBLANES, B_D_INNER))


def _ssd_scan_kernel(s_ref, da_ref, h_ref, *, ncc, nch):
    order_f = list(range(nch))
    order_b = list(range(ncc - 1, -1, -1)) + list(range(nch - 1, ncc - 1, -1))
    for d, order in enumerate((order_f, order_b)):
        h = jnp.zeros((B_STATE, B_D_INNER), F32)
        for c in order:
            h_ref[0, c, d] = h
            h = da_ref[0, c, d, 0:1, :] * h + s_ref[0, c, d]


def _ssd_out_kernel(prev_ref, cur_ref, next_ref, cw_ref, cb_ref, dtb_ref, alog_ref, z_ref, h_ref, dsk_ref,
                    nw_ref, o_ref, *, ncc, nch):
    c = pl.program_id(1)
    xs, bm, cm, cum, dtx, cumx, aend = _ssd_prep(prev_ref, cur_ref, next_ref, cw_ref, cb_ref, dtb_ref,
                                                 alog_ref, c, ncc, nch)
    half = B_D_INNER // 2
    lane_s = _lane_iota((CH, LANES))
    lane_h = _lane_iota((B_STATE, B_D_INNER))
    lane_y = _lane_iota((CH, half)) // B_HEADDIM
    cmb = cm.astype(BF16)
    bmb = bm.astype(BF16)
    zero = jnp.zeros_like(cmb)
    gmat = [_dot_nt(jnp.where((lane_s // B_STATE) == g, cmb, zero), bmb) for g in range(B_GROUPS)]
    cumt = cum.T
    li = _row_iota((CH, CH))
    si = _lane_iota((CH, CH))
    y = xs * (dsk_ref[0:1, :] + dsk_ref[1:2, :])
    for d in range(2):
        tri = (si <= li) if d == 0 else (si >= li)
        hc = h_ref[0, 0, d]
        h2 = jnp.concatenate([jnp.where(lane_h < half, hc, 0.0), jnp.where(lane_h < half, 0.0, hc)], axis=0)
        y = y + _dot(cmb, h2.astype(BF16)) * jnp.exp(cumx[d])
        xdt = (xs * dtx[d]).astype(BF16)
        parts = []
        for g in range(B_GROUPS):
            acc = jnp.zeros((CH, half), F32)
            for hl in range(B_HEADS // B_GROUPS):
                idx = d * B_HEADS + g * (B_HEADS // B_GROUPS) + hl
                seg = cum[:, idx:idx + 1] - cumt[idx:idx + 1, :]
                dec = jnp.where(tri, jnp.exp(jnp.where(tri, seg, 0.0)), 0.0)
                sc = (gmat[g] * dec).astype(BF16)
                yh = _dot(sc, xdt[:, g * half:(g + 1) * half])
                acc = jnp.where(lane_y == hl, yh, acc)
            parts.append(acc)
        y = y + jnp.concatenate(parts, axis=1)
    gz = y * _silu(z_ref[0])
    outs = []
    for g in range(B_GROUPS):
        gg = gz[:, g * half:(g + 1) * half]
        ms = jnp.mean(gg * gg, axis=-1, keepdims=True)
        outs.append(gg * lax.rsqrt(ms + EPS))
    o_ref[0] = (jnp.concatenate(outs, axis=1) * nw_ref[...]).astype(BF16)


def _ssd(z, xb, conv_w, conv_b, dt_bias, a_log, d_skip, norm_w, lc):
    B, S, _ = z.shape
    nch, ncc = S // CH, lc // CH
    pad = lambda v: jnp.pad(v.reshape(1, -1), ((0, 0), (0, LANES - v.size)))
    consts = [conv_w, conv_b.reshape(1, XBC_W), pad(dt_bias), pad(a_log)]
    specs = _ssd_specs(S, B)
    st_spec = pl.BlockSpec((1, 1, 2, B_STATE, B_D_INNER), lambda b, c: (b, c, 0, 0, 0))
    da_spec = pl.BlockSpec((1, 1, 2, SUBLANES, B_D_INNER), lambda b, c: (b, c, 0, 0, 0))
    st_shape = jax.ShapeDtypeStruct((B, nch, 2, B_STATE, B_D_INNER), F32)
    da_shape = jax.ShapeDtypeStruct((B, nch, 2, SUBLANES, B_D_INNER), F32)
    small = 16 * CH * XB_W * 4
    states, da = pl.pallas_call(
        functools.partial(_ssd_state_kernel, ncc=ncc, nch=nch),
        grid=(B, nch), in_specs=specs, out_specs=[st_spec, da_spec], out_shape=[st_shape, da_shape],
        compiler_params=_cparams(("parallel", "parallel"), small),
        name="ssd_state",
    )(xb, xb, xb, *consts)
    st_bytes = nch * 2 * B_STATE * B_D_INNER * 4
    hstart = pl.pallas_call(
        functools.partial(_ssd_scan_kernel, ncc=ncc, nch=nch),
        grid=(B,),
        in_specs=[pl.BlockSpec((1, nch, 2, B_STATE, B_D_INNER), lambda b: (b, 0, 0, 0, 0)),
                  pl.BlockSpec((1, nch, 2, SUBLANES, B_D_INNER), lambda b: (b, 0, 0, 0, 0))],
        out_specs=pl.BlockSpec((1, nch, 2, B_STATE, B_D_INNER), lambda b: (b, 0, 0, 0, 0)),
        out_shape=st_shape,
        compiler_params=_cparams(("parallel",), 5 * st_bytes // 2),
        name="ssd_scan",
    )(states, da)
    dsk = jnp.repeat(d_skip, B_HEADDIM, axis=1)
    return pl.pallas_call(
        functools.partial(_ssd_out_kernel, ncc=ncc, nch=nch),
        grid=(B, nch),
        in_specs=specs + [pl.BlockSpec((1, CH, B_D_INNER), lambda b, c: (b, c, 0)), st_spec,
                          pl.BlockSpec((2, B_D_INNER), lambda b, c: (0, 0)),
                          pl.BlockSpec((1, B_D_INNER), lambda b, c: (0, 0))],
        out_specs=pl.BlockSpec((1, CH, B_D_INNER), lambda b, c: (b, c, 0)),
        out_shape=jax.ShapeDtypeStruct((B, S, B_D_INNER), BF16),
        compiler_params=_cparams(("parallel", "parallel"), small),
        name="ssd_out",
    )(xb, xb, xb, *consts, z, hstart, dsk, norm_w.reshape(1, B_D_INNER))


def _outproj_kernel(oa_ref, ob_ref, oc_ref, x_ref, mod_ref, n2_ref, w_ref, wr_ref, xo_ref, h2_ref, rt_ref,
                    cnt_ref):
    @pl.when((pl.program_id(0) == 0) & (pl.program_id(1) == 0))
    def _():
        cnt_ref[...] = jnp.zeros_like(cnt_ref)

    a_w = A_HEADS * A_HEAD_DIM
    mix = (_dot(oa_ref[0], w_ref[0:a_w, :]) + _dot(ob_ref[0], w_ref[a_w:a_w + B_D_INNER, :])
           + _dot(oc_ref[0], w_ref[a_w + B_D_INNER:, :]))
    x = x_ref[0] + mod_ref[0, 2:3, :] * mix
    xo_ref[0] = x
    ms = jnp.mean(x * x, axis=-1, keepdims=True)
    h2 = x * lax.rsqrt(ms + EPS) * n2_ref[...]
    h2 = h2 * (1.0 + mod_ref[0, 4:5, :]) + mod_ref[0, 3:4, :]
    h2_ref[0] = h2
    h_hi, h_mid, _ = _split3(h2)
    logit = _dot(jnp.concatenate([h_hi, h_hi, h_mid], axis=1), wr_ref[...])
    lf = _lane_iota(logit.shape).astype(F32)
    big = 1e9
    gmask = lf < N_GROUPS
    gl = jnp.where(gmask, logit, NEG)
    gm = jnp.max(gl, axis=-1, keepdims=True)
    g_p = 1.0 / jnp.sum(jnp.exp(gl - gm), axis=-1, keepdims=True)
    g_sel = jnp.min(jnp.where(gmask & (gl == gm), lf, big), axis=-1, keepdims=True)
    lo = N_GROUPS + EXPERTS_PER_GROUP * g_sel
    emask = (lf >= lo) & (lf < lo + EXPERTS_PER_GROUP)
    el = jnp.where(emask, logit, NEG)
    v1 = jnp.max(el, axis=-1, keepdims=True)
    i1 = jnp.min(jnp.where(emask & (el == v1), lf, big), axis=-1, keepdims=True)
    rest = emask & (lf != i1)
    el2 = jnp.where(rest, logit, NEG)
    v2 = jnp.max(el2, axis=-1, keepdims=True)
    i2 = jnp.min(jnp.where(rest & (el2 == v2), lf, big), axis=-1, keepdims=True)
    t = jnp.exp(v2 - v1)
    gate1 = g_p / (1.0 + t)
    gate2 = g_p * t / (1.0 + t)
    oh1 = lf == i1 - N_GROUPS
    oh2 = lf == i2 - N_GROUPS
    oh = jnp.where(oh1 | oh2, 1.0, 0.0)
    earlier = (_lane_iota((TM, TM)) < _row_iota((TM, TM))).astype(BF16)
    prefix = _dot(earlier, oh.astype(BF16)) + cnt_ref[0:1, :]
    rank1 = jnp.sum(jnp.where(oh1, prefix, 0.0), axis=-1, keepdims=True)
    rank2 = jnp.sum(jnp.where(oh2, prefix, 0.0), axis=-1, keepdims=True)
    cnt_ref[...] = cnt_ref[...] + jnp.sum(oh, axis=0, keepdims=True)
    info = jnp.zeros_like(logit)
    for k, v in enumerate((i1 - N_GROUPS, i2 - N_GROUPS, gate1, gate2, rank1, rank2)):
        info = jnp.where(lf == k, v, info)
    rt_ref[0] = info


RT_E, RT_GATE, RT_RANK = 0, 2, 4


def _outproj(oa, ob, oc, x, mod_l, n2w, w_out, w_r):
    B, S, D = x.shape
    row = lambda w: pl.BlockSpec((1, TM, w), lambda b, j: (b, j, 0))
    mixw = w_out.shape[0]
    return pl.pallas_call(
        _outproj_kernel,
        grid=(B, S // TM),
        in_specs=[row(oa.shape[2]), row(ob.shape[2]), row(oc.shape[2]), row(D),
                  pl.BlockSpec((1, 6, D), _mod_index(B)), pl.BlockSpec((1, D), lambda b, j: (0, 0)),
                  pl.BlockSpec((mixw, D), lambda b, j: (0, 0)),
                  pl.BlockSpec((3 * D, LANES), lambda b, j: (0, 0))],
        out_specs=[row(D), row(D), row(LANES), pl.BlockSpec((SUBLANES, LANES), lambda b, j: (0, 0))],
        out_shape=[jax.ShapeDtypeStruct((B, S, D), F32), jax.ShapeDtypeStruct((B, S, D), F32),
                   jax.ShapeDtypeStruct((B, S, LANES), F32), jax.ShapeDtypeStruct((SUBLANES, LANES), F32)],
        compiler_params=_cparams(("arbitrary", "arbitrary"), 2 * mixw * D * 2 + 14 * TM * D * 4),
        name="outproj",
    )(oa, ob, oc, x, mod_l, n2w.reshape(1, D), w_out, w_r)


def _expert_kernel(be_ref, nu_ref, nxt_ref, slot_ref, x_ref, wg_hbm, wu_hbm, wd_hbm, o_ref,
                   wgf_ref, wuf_ref, wdf_ref, wgb_ref, wub_ref, wdb_ref, sem, *, layer):
    i = pl.program_id(0)
    used = i < nu_ref[0]

    def weight_copies(e, s):
        return [pltpu.make_async_copy(src.at[layer, e], dst.at[s], sem.at[s, n])
                for n, (src, dst) in enumerate(((wg_hbm, wgf_ref), (wu_hbm, wuf_ref), (wd_hbm, wdf_ref)))]

    @pl.when(i == 0)
    def _():
        for cp in weight_copies(be_ref[0], 0):
            cp.start()

    @pl.when(used & (nxt_ref[i] >= -1))
    def _():
        s = slot_ref[i]
        for cp in weight_copies(be_ref[i], s):
            cp.wait()

        @pl.when(nxt_ref[i] >= 0)
        def _():
            for cp in weight_copies(nxt_ref[i], 1 - s):
                cp.start()

        wgb_ref[...] = wgf_ref[s].astype(BF16)
        wub_ref[...] = wuf_ref[s].astype(BF16)
        wdb_ref[...] = wdf_ref[s].astype(BF16)

    @pl.when(used)
    def _():
        xb = x_ref[...].astype(BF16)
        hid = _silu(_dot(xb, wgb_ref[...])) * _dot(xb, wub_ref[...])
        o_ref[...] = _dot(hid.astype(BF16), wdb_ref[...])

    @pl.when(jnp.logical_not(used))
    def _():
        o_ref[...] = jnp.zeros_like(o_ref)


def _expert_mlp(buf, block_e, n_used, w_gate, w_up, w_down, layer):
    rows, D = buf.shape
    de = w_gate.shape[3]
    nblk = rows // MOE_ROWS
    idx = jnp.arange(nblk)
    first = (idx < n_used[0]) & ((idx == 0) | (block_e != jnp.roll(block_e, 1)))
    slot = ((jnp.cumsum(first) - 1) % 2).astype(jnp.int32)
    first_pos = jnp.where(first, idx, nblk)
    after = jnp.concatenate([lax.cummin(first_pos, reverse=True)[1:], jnp.full((1,), nblk)])
    nxt_e = jnp.sum(jnp.where(after[:, None] == idx[None, :], block_e[None, :] + 1, 0), axis=1) - 1
    nxt = jnp.where(first, nxt_e, -2).astype(jnp.int32)
    hbm = pl.BlockSpec(memory_space=pl.ANY)
    grid_spec = pltpu.PrefetchScalarGridSpec(
        num_scalar_prefetch=4, grid=(nblk,),
        in_specs=[pl.BlockSpec((MOE_ROWS, D), lambda i, be, nu, nx, sl: (jnp.minimum(i, nu[0] - 1), 0)),
                  hbm, hbm, hbm],
        out_specs=pl.BlockSpec((MOE_ROWS, D), lambda i, be, nu, nx, sl: (i, 0)),
        scratch_shapes=[pltpu.VMEM((2, D, de), F32), pltpu.VMEM((2, D, de), F32), pltpu.VMEM((2, de, D), F32),
                        pltpu.VMEM((D, de), BF16), pltpu.VMEM((D, de), BF16), pltpu.VMEM((de, D), BF16),
                        pltpu.SemaphoreType.DMA((2, 3))])
    return pl.pallas_call(
        functools.partial(_expert_kernel, layer=layer), grid_spec=grid_spec,
        out_shape=jax.ShapeDtypeStruct((rows, D), F32),
        compiler_params=_cparams(("arbitrary",), 2 * 3 * D * de * 4 + 3 * D * de * 2 + 8 * MOE_ROWS * D * 4),
        name="expert_mlp",
    )(block_e, n_used, nxt, slot, buf, w_gate, w_up, w_down)


def _dispatch_plan(route, counts):
    T = route.shape[0]
    n_blocks = -(-T * TOP_K // MOE_ROWS) + N_EXPERTS
    cnt = counts[0, :N_EXPERTS].astype(jnp.int32)
    nb = (cnt + MOE_ROWS - 1) // MOE_ROWS
    blk_end = jnp.cumsum(nb)
    row_start = (blk_end - nb) * MOE_ROWS
    e = route[:, RT_E:RT_E + TOP_K].astype(jnp.int32)
    rank = route[:, RT_RANK:RT_RANK + TOP_K].astype(jnp.int32)
    table = jnp.where(e[:, :, None] == jnp.arange(N_EXPERTS)[None, None, :], row_start[None, None, :], 0)
    pos = (jnp.sum(table, axis=-1) + rank).reshape(T // TM, 1, TM * TOP_K)
    block_e = jnp.sum(blk_end[None, :] <= jnp.arange(n_blocks)[:, None], axis=1)
    block_e = jnp.minimum(block_e, N_EXPERTS - 1).astype(jnp.int32)
    n_used = blk_end[-1].astype(jnp.int32).reshape(1)
    tails = jnp.concatenate([row_start + cnt, nb * MOE_ROWS - cnt, n_used]).astype(jnp.int32)
    return pos.astype(jnp.int32), block_e, n_used, n_blocks, tails


def _zero_tails(tails_ref, zero_ref, buf_ref, sem, op):
    for e in range(N_EXPERTS):
        start, length = tails_ref[e], tails_ref[N_EXPERTS + e]
        end = start + length
        bit = MOE_ROWS // 2
        while bit >= SUBLANES:
            done = (length // (2 * bit)) * (2 * bit)

            @pl.when((length & bit) != 0)
            def _(bit=bit, done=done, end=end):
                at = pl.multiple_of(end - done - bit, SUBLANES)
                cp = pltpu.make_async_copy(zero_ref.at[pl.ds(0, bit)], buf_ref.at[pl.ds(at, bit)], sem)
                cp.start() if op == "start" else cp.wait()

            bit //= 2
        for r in range(SUBLANES - 1):
            @pl.when(r < (length & (SUBLANES - 1)))
            def _(r=r, start=start):
                cp = _row_copy(zero_ref, 0, buf_ref, start + r, sem)
                cp.start() if op == "start" else cp.wait()
    half = zero_ref.shape[0]
    for blk in range(buf_ref.shape[0] // MOE_ROWS):
        @pl.when(blk >= tails_ref[2 * N_EXPERTS])
        def _(blk=blk):
            for at in range(blk * MOE_ROWS, (blk + 1) * MOE_ROWS, half):
                cp = pltpu.make_async_copy(zero_ref, buf_ref.at[pl.ds(at, half)], sem)
                cp.start() if op == "start" else cp.wait()


def _scatter_rows(h2_ref, block, pos_ref, buf_ref, sem, op):
    for r in range(TM):
        for k in range(TOP_K):
            cp = _row_copy(h2_ref, block * TM + r, buf_ref, pos_ref[0, 0, TOP_K * r + k], sem)
            cp.start() if op == "start" else cp.wait()


def _dispatch_kernel(tails_ref, pos_ref, posp_ref, h2_ref, buf_ref, zero_ref, sem, zsem):
    i = pl.program_id(0)
    slot = i % 2

    @pl.when(i == 0)
    def _():
        zero_ref[...] = jnp.zeros_like(zero_ref)
        _zero_tails(tails_ref, zero_ref, buf_ref, zsem, "start")
        _zero_tails(tails_ref, zero_ref, buf_ref, zsem, "wait")

    _scatter_rows(h2_ref, i, pos_ref, buf_ref, sem.at[slot], "start")

    @pl.when(i > 0)
    def _():
        _scatter_rows(h2_ref, i - 1, posp_ref, buf_ref, sem.at[1 - slot], "wait")

    @pl.when(i == pl.num_programs(0) - 1)
    def _():
        _scatter_rows(h2_ref, i, pos_ref, buf_ref, sem.at[slot], "wait")


def _dispatch(h2, pos, tails, n_blocks):
    T, D = h2.shape
    posspec = lambda f: pl.BlockSpec((1, 1, TM * TOP_K), lambda i, tl: (f(i), 0, 0), memory_space=pltpu.SMEM)
    grid_spec = pltpu.PrefetchScalarGridSpec(
        num_scalar_prefetch=1, grid=(T // TM,),
        in_specs=[posspec(lambda i: i), posspec(lambda i: jnp.maximum(i - 1, 0)),
                  pl.BlockSpec(memory_space=pl.ANY)],
        out_specs=pl.BlockSpec(memory_space=pl.ANY),
        scratch_shapes=[pltpu.VMEM((MOE_ROWS // 2, D), F32), pltpu.SemaphoreType.DMA((2,)),
                        pltpu.SemaphoreType.DMA])
    return pl.pallas_call(
        _dispatch_kernel, grid_spec=grid_spec,
        out_shape=jax.ShapeDtypeStruct((n_blocks * MOE_ROWS, D), F32),
        compiler_params=_cparams(("arbitrary",), 6 * TM * D * 4),
        name="moe_dispatch",
    )(tails, pos, pos, h2)


def _final_kernel(pos_ref, posn_ref, x_ref, rt_ref, eo_ref, modp_ref, nw_ref, o_ref, rows_ref, sem):
    slot = pl.program_id(0) % 2

    @pl.when(pl.program_id(0) == 0)
    def _():
        _gather_rows(eo_ref, pos_ref, rows_ref.at[0], sem.at[0], "start")
        _gather_rows(eo_ref, pos_ref, rows_ref.at[0], sem.at[0], "wait")

    _gather_rows(eo_ref, posn_ref, rows_ref.at[1 - slot], sem.at[1 - slot], "start")
    rt = rt_ref[0]
    y = rt[:, RT_GATE:RT_GATE + 1] * rows_ref[slot, 0] + rt[:, RT_GATE + 1:RT_GATE + 2] * rows_ref[slot, 1]
    x = x_ref[0] + modp_ref[0, 5:6, :] * y
    ms = jnp.mean(x * x, axis=-1, keepdims=True)
    o_ref[0] = x * lax.rsqrt(ms + EPS) * nw_ref[...]
    _gather_rows(eo_ref, posn_ref, rows_ref.at[1 - slot], sem.at[1 - slot], "wait")


def _final(x, moe, mod_prev, nw, lc):
    eo, pos, route = moe
    B, S, D = x.shape
    off, nblk = lc // TM, S // TM
    nlat = nblk - off
    nsteps = B * nlat
    blk = lambda i: (i // nlat) * nblk + i % nlat + off
    row = lambda w: pl.BlockSpec((1, TM, w), lambda i: (i // nlat, i % nlat + off, 0))
    posspec = lambda f: pl.BlockSpec((1, 1, TM * TOP_K), lambda i: (f(i), 0, 0), memory_space=pltpu.SMEM)
    return pl.pallas_call(
        _final_kernel,
        grid=(nsteps,),
        in_specs=[posspec(blk), posspec(lambda i: blk(jnp.minimum(i + 1, nsteps - 1))), row(D), row(LANES),
                  pl.BlockSpec(memory_space=pl.ANY), pl.BlockSpec((1, 6, D), lambda i: (i // nlat, 0, 0)),
                  pl.BlockSpec((1, D), lambda i: (0, 0))],
        out_specs=pl.BlockSpec((1, TM, D), lambda i: (i // nlat, i % nlat, 0)),
        out_shape=jax.ShapeDtypeStruct((B, S - lc, D), F32),
        scratch_shapes=[pltpu.VMEM((2, TOP_K, TM, D), F32), pltpu.SemaphoreType.DMA((2,))],
        compiler_params=_cparams(("arbitrary",), 10 * TM * D * 4),
        name="final_norm",
    )(pos, pos, x, route.reshape(B, S, LANES), eo, mod_prev, nw.reshape(1, D))


def _pack_w_in(w):
    d = w.shape[0]
    aq, ak, av = w[:, 0:256], w[:, 256:384], w[:, 384:512]
    dup = lambda m: jnp.concatenate([m[:, 0:64], m[:, 0:64], m[:, 64:128], m[:, 64:128]], axis=1)
    bz, bx, bdt = w[:, 512:1024], w[:, 1024:1792], w[:, 1792:1808]
    cq, ck, cv = w[:, 1808:2064], w[:, 2064:2320], w[:, 2320:2576]
    packed = jnp.concatenate([aq, dup(ak), dup(av), bz, bx, bdt, jnp.zeros((d, LANES - 16), w.dtype),
                              cq, ck, cv], axis=1)
    return packed.astype(BF16)


def kernel(x, c, ctx, c_ctx, w_mod, b_mod, norm1_w, norm2_w, w_in, w_out, a_sink, b_conv_w, b_conv_b,
           b_dt_bias, b_a_log, b_d, b_norm_w, c_lambda, c_subln_w, moe_group_router, moe_router,
           moe_w_gate, moe_w_up, moe_w_down, final_norm_w):
    B, L, D = x.shape
    Lc = ctx.shape[1]
    depth = w_mod.shape[0]
    assert Lc == TM and L % TM == 0 and B + 1 <= SUBLANES
    S = Lc + L
    T = B * S
    xa = jnp.concatenate([ctx, x], axis=1)
    c_all = jnp.concatenate([c, c_ctx[None, :], jnp.zeros((SUBLANES - B - 1, D), F32)], axis=0)
    mod = _mod_table(c_all, w_mod, b_mod)
    tabs = _rope_tables(L, Lc, A_HEAD_DIM) + _rope_tables(L, Lc, C_QK_DIM)
    moe, mod_prev = None, None
    for l in range(depth):
        mod_l = mod[l].reshape(SUBLANES, 6, D)
        lambda_init = 0.8 - 0.6 * math.exp(-0.3 * l)
        outs = _inproj(xa, moe, mod_prev, mod_l, norm1_w[l], _pack_w_in(w_in[l]), tabs)
        if moe is not None:
            xa, outs = outs[0], outs[1:]
        qa, ka, va, z, xb, qc, kc, vc = outs
        oa = _attn_a(a_sink[l], qa, ka, va, Lc)
        ob = _ssd(z, xb, b_conv_w[l], b_conv_b[l], b_dt_bias[l], b_a_log[l], b_d[l], b_norm_w[l], Lc)
        oc = _attn_c(qc, kc, vc, c_lambda[l], c_subln_w[l], Lc, lambda_init)
        w_r = jnp.concatenate([moe_group_router[l], moe_router[l],
                               jnp.zeros((D, LANES - N_GROUPS - N_EXPERTS), F32)], axis=1)
        w_hi = w_r.astype(BF16)
        w_mid = (w_r - w_hi.astype(F32)).astype(BF16)
        xa, h2, route, counts = _outproj(oa, ob, oc, xa, mod_l, norm2_w[l], w_out[l].astype(BF16),
                                         jnp.concatenate([w_hi, w_mid, w_hi], axis=0))
        route = route.reshape(T, LANES)
        pos, block_e, n_used, n_blocks, tails = _dispatch_plan(route, counts)
        buf = _dispatch(h2.reshape(T, D), pos, tails, n_blocks)
        eo = _expert_mlp(buf, block_e, n_used, moe_w_gate, moe_w_up, moe_w_down, l)
        moe, mod_prev = (eo, pos, route), mod_l
    return _final(xa, moe, mod_prev, final_norm_w, Lc)
```

```python
import functools
import math

import jax
import jax.numpy as jnp
from jax import lax
from jax.experimental import pallas as pl
from jax.experimental.pallas import tpu as pltpu

F32 = jnp.float32
BF16 = jnp.bfloat16
HI = lax.Precision.HIGHEST

GRID_W = 64
EPS = 1e-6
ROPE_BASE = 10000.0

A_HEADS = 4
A_KV_HEADS = 2
A_HEAD_DIM = 64
A_WINDOW = 128
B_D_INNER = 512
B_HEADDIM = 64
B_HEADS = B_D_INNER // B_HEADDIM
B_GROUPS = 2
B_STATE = 64
B_CONV = 5
C_HEADS = 4
C_QK_DIM = 32
C_V_DIM = 64
N_GROUPS = 4
EXPERTS_PER_GROUP = 8
N_EXPERTS = N_GROUPS * EXPERTS_PER_GROUP
TOP_K = 2

LANES = 128
SUBLANES = 8
TM = 256
CH = 128
MOE_ROWS = 512
NEG = -1e30
VMEM_CAP = 64 * 1024 * 1024

P_AQ = 0
P_AK = 256
P_AV = 512
P_BZ = 768
P_BX = 1280
P_BDT = 2048
P_CQ = 2176
P_CK = 2432
P_CV = 2688
P_COLS = 2944


def _cparams(sem, vmem_bytes):
    limit = int(min(max(2 * vmem_bytes, 16 * 1024 * 1024), VMEM_CAP - 8 * 1024 * 1024))
    return pltpu.CompilerParams(dimension_semantics=sem, vmem_limit_bytes=limit)


def _lane_iota(shape):
    return lax.broadcasted_iota(jnp.int32, shape, len(shape) - 1)


def _row_iota(shape):
    return lax.broadcasted_iota(jnp.int32, shape, len(shape) - 2)


def _silu(v):
    return v * (1.0 / (1.0 + jnp.exp(-v)))


def _dot(a, b):
    return jnp.dot(a, b, preferred_element_type=F32)


def _dot_nt(a, b):
    return lax.dot_general(a, b, (((1,), (1,)), ((), ())), preferred_element_type=F32)


def _mod_kernel(c_ref, w_ref, b_ref, o_ref):
    s = _silu(c_ref[...]).astype(BF16)
    o_ref[0] = _dot(s, w_ref[0].astype(BF16)) + b_ref[0]


def _mod_table(c_all, w_mod, b_mod):
    depth, d, d6 = w_mod.shape
    tn = 1536
    return pl.pallas_call(
        _mod_kernel,
        grid=(depth, d6 // tn),
        in_specs=[
            pl.BlockSpec((SUBLANES, d), lambda l, n: (0, 0)),
            pl.BlockSpec((1, d, tn), lambda l, n: (l, 0, n)),
            pl.BlockSpec((1, 1, tn), lambda l, n: (l, 0, n)),
        ],
        out_specs=pl.BlockSpec((1, SUBLANES, tn), lambda l, n: (l, 0, n)),
        out_shape=jax.ShapeDtypeStruct((depth, SUBLANES, d6), F32),
        compiler_params=_cparams(("parallel", "parallel"), 2 * d * tn * 4 + d * tn * 2),
        name="mod_table",
    )(c_all, w_mod, b_mod.reshape(depth, 1, d6))


def _rope_tables(L, Lc, dim):
    half = dim // 2
    nf = half // 2
    rows = L // GRID_W
    row_pos = jnp.repeat(jnp.arange(rows, dtype=jnp.int32), GRID_W).astype(F32)
    col_pos = jnp.tile(jnp.arange(GRID_W, dtype=jnp.int32), rows).astype(F32)
    freqs = ROPE_BASE ** (-jnp.arange(nf, dtype=F32) / nf)
    d = jnp.arange(dim)
    within = d % half
    f = freqs[within % nf]
    first = within < nf
    pos = jnp.where((d // half == 0)[None, :], row_pos[:, None], col_pos[:, None])
    ang = pos * f[None, :]
    cos = jnp.cos(ang)
    sin = jnp.where(first[None, :], -jnp.sin(ang), jnp.sin(ang))
    cos = jnp.concatenate([jnp.ones((Lc, dim), F32), cos], axis=0)
    sin = jnp.concatenate([jnp.zeros((Lc, dim), F32), sin], axis=0)
    reps = LANES // dim
    return jnp.tile(cos, (1, reps)), jnp.tile(sin, (1, reps))


def _rope(t, cos, sin, nf):
    w = t.shape[1]
    reps = w // LANES
    cosw = jnp.concatenate([cos] * reps, axis=1) if reps > 1 else cos
    sinw = jnp.concatenate([sin] * reps, axis=1) if reps > 1 else sin
    lane = _lane_iota(t.shape)
    first = (lane & (2 * nf - 1)) < nf
    partner = jnp.where(first, pltpu.roll(t, w - nf, axis=1), pltpu.roll(t, nf, axis=1))
    return t * cosw + partner * sinw


def _row_copy(src, i, dst, j, sem):
    return pltpu.make_async_copy(src.at[pl.ds(i, 1)], dst.at[pl.ds(j, 1)], sem)


def _gather_rows(eo_ref, pos_ref, dst_ref, sem, op):
    for r in range(TM):
        for k in range(TOP_K):
            cp = _row_copy(eo_ref, pos_ref[0, 0, TOP_K * r + k], dst_ref.at[k], r, sem)
            cp.start() if op == "start" else cp.wait()


def _inproj_kernel(*refs, has_moe):
    if not has_moe:
        x_ref, rest = refs[0], refs[1:]
        _inproj_compute(x_ref[0], *rest)
        return
    (pos_ref, posn_ref, x_ref, rt_ref, eo_ref, modp_ref, mod_ref, n1_ref, w_ref, cosa_ref, sina_ref,
     cosc_ref, sinc_ref, xo_ref, qa_ref, ka_ref, va_ref, z_ref, xb_ref, qc_ref, kc_ref, vc_ref,
     rows_ref, sem) = refs
    slot = pl.program_id(0) % 2

    @pl.when(pl.program_id(0) == 0)
    def _():
        _gather_rows(eo_ref, pos_ref, rows_ref.at[0], sem.at[0], "start")
        _gather_rows(eo_ref, pos_ref, rows_ref.at[0], sem.at[0], "wait")

    _gather_rows(eo_ref, posn_ref, rows_ref.at[1 - slot], sem.at[1 - slot], "start")
    rt = rt_ref[0]
    y = rt[:, RT_GATE:RT_GATE + 1] * rows_ref[slot, 0] + rt[:, RT_GATE + 1:RT_GATE + 2] * rows_ref[slot, 1]
    x = x_ref[0] + modp_ref[0, 5:6, :] * y
    xo_ref[0] = x
    _inproj_compute(x, mod_ref, n1_ref, w_ref, cosa_ref, sina_ref, cosc_ref, sinc_ref,
                    qa_ref, ka_ref, va_ref, z_ref, xb_ref, qc_ref, kc_ref, vc_ref)
    _gather_rows(eo_ref, posn_ref, rows_ref.at[1 - slot], sem.at[1 - slot], "wait")


def _inproj_compute(x, mod_ref, n1_ref, w_ref, cosa_ref, sina_ref, cosc_ref, sinc_ref,
                    qa_ref, ka_ref, va_ref, z_ref, xb_ref, qc_ref, kc_ref, vc_ref):
    ms = jnp.mean(x * x, axis=-1, keepdims=True)
    h = x * lax.rsqrt(ms + EPS) * n1_ref[...]
    h = h * (1.0 + mod_ref[0, 1:2, :]) + mod_ref[0, 0:1, :]
    hb = h.astype(BF16)

    def proj(lo, hi):
        return _dot(hb, w_ref[:, lo:hi])

    cosa, sina = cosa_ref[...], sina_ref[...]
    cosc, sinc = cosc_ref[...], sinc_ref[...]
    nfa = A_HEAD_DIM // 4
    nfc = C_QK_DIM // 4
    qa_ref[0] = (_rope(proj(P_AQ, P_AK), cosa, sina, nfa) * (A_HEAD_DIM ** -0.5)).astype(BF16)
    ka_ref[0] = _rope(proj(P_AK, P_AV), cosa, sina, nfa).astype(BF16)
    va_ref[0] = proj(P_AV, P_BZ).astype(BF16)
    z_ref[0] = proj(P_BZ, P_BX)
    xb_ref[0] = proj(P_BX, P_CQ)
    qc_ref[0] = (_rope(proj(P_CQ, P_CK), cosc, sinc, nfc) * (C_QK_DIM ** -0.5)).astype(BF16)
    kc_ref[0] = _rope(proj(P_CK, P_CV), cosc, sinc, nfc).astype(BF16)
    vc_ref[0] = proj(P_CV, P_COLS).astype(BF16)


def _mod_index(nb):
    return lambda b, j: (jnp.where(j == 0, nb, b), 0, 0)


def _inproj(x, moe, mod_prev, mod_l, n1w, w_p, tabs):
    B, S, D = x.shape
    nblk = S // TM
    nsteps = B * nblk
    row = lambda w: pl.BlockSpec((1, TM, w), lambda i: (i // nblk, i % nblk, 0))
    tab = pl.BlockSpec((TM, LANES), lambda i: (i % nblk, 0))
    modspec = pl.BlockSpec((1, 6, D), lambda i: (jnp.where(i % nblk == 0, B, i // nblk), 0, 0))
    const = lambda r, c: pl.BlockSpec((r, c), lambda i: (0, 0))
    has_moe = moe is not None
    in_specs, args, scratch = [], [], []
    if has_moe:
        eo, pos, route = moe
        posspec = lambda f: pl.BlockSpec((1, 1, TM * TOP_K), lambda i: (f(i), 0, 0), memory_space=pltpu.SMEM)
        in_specs += [posspec(lambda i: i), posspec(lambda i: jnp.minimum(i + 1, nsteps - 1)), row(D),
                     row(LANES), pl.BlockSpec(memory_space=pl.ANY), modspec]
        args += [pos, pos, x, route.reshape(B, S, LANES), eo, mod_prev]
        scratch = [pltpu.VMEM((2, TOP_K, TM, D), F32), pltpu.SemaphoreType.DMA((2,))]
    else:
        in_specs += [row(D)]
        args += [x]
    in_specs += [modspec, const(1, D), const(D, P_COLS), tab, tab, tab, tab]
    args += [mod_l, n1w.reshape(1, D), w_p, *tabs]
    widths = [(256, BF16), (256, BF16), (256, BF16), (512, F32), (896, F32), (256, BF16), (256, BF16),
              (256, BF16)]
    out_specs = [row(w) for w, _ in widths]
    out_shape = [jax.ShapeDtypeStruct((B, S, w), dt) for w, dt in widths]
    if has_moe:
        out_specs = [row(D)] + out_specs
        out_shape = [jax.ShapeDtypeStruct((B, S, D), F32)] + out_shape
    vmem = 2 * D * P_COLS * 2 + 12 * TM * D * 4 + 4 * TM * P_COLS * 4
    return pl.pallas_call(
        functools.partial(_inproj_kernel, has_moe=has_moe),
        grid=(nsteps,), in_specs=in_specs, out_specs=out_specs, out_shape=out_shape, scratch_shapes=scratch,
        compiler_params=_cparams(("arbitrary",), vmem),
        name="inproj",
    )(*args)


def _attn_a_kernel(sink_ref, q_ref, k_ref, v_ref, o_ref, *, lc):
    j = pl.program_id(1)
    S = k_ref.shape[1]
    span = TM + 2 * A_WINDOW
    start = jnp.clip(j * TM - A_WINDOW, 0, S - span)
    start = pl.multiple_of(start, LANES)
    qrow = j * TM + _row_iota((TM, span))
    krow = start + _lane_iota((TM, span))
    mask = (jnp.abs(krow - qrow) <= A_WINDOW) & (krow >= lc) & (j > 0)
    mask2 = jnp.concatenate([mask, mask], axis=0)
    lane = _lane_iota((TM, LANES))
    low = lane < A_HEAD_DIM
    top = _row_iota((2 * TM, 1)) < TM
    for c in range(A_KV_HEADS):
        cs = slice(c * LANES, (c + 1) * LANES)
        q = q_ref[0, :, cs]
        zero = jnp.zeros_like(q)
        qs = jnp.concatenate([jnp.where(low, q, zero), jnp.where(low, zero, q)], axis=0)
        kl = k_ref[0, pl.ds(start, span), cs]
        vl = v_ref[0, pl.ds(start, span), cs]
        kc = k_ref[0, 0:lc, cs]
        vc = v_ref[0, 0:lc, cs]
        s_l = jnp.where(mask2, _dot_nt(qs, kl), NEG)
        s_c = _dot_nt(qs, kc)
        sink = jnp.where(top, sink_ref[2 * c], sink_ref[2 * c + 1])
        m = jnp.maximum(jnp.maximum(jnp.max(s_l, axis=-1, keepdims=True),
                                    jnp.max(s_c, axis=-1, keepdims=True)), sink)
        p_l = jnp.exp(s_l - m)
        p_c = jnp.exp(s_c - m)
        den = (jnp.sum(p_l, axis=-1, keepdims=True) + jnp.sum(p_c, axis=-1, keepdims=True)
               + jnp.exp(sink - m))
        o = (_dot(p_l.astype(BF16), vl) + _dot(p_c.astype(BF16), vc)) * (1.0 / den)
        o_ref[0, :, cs] = jnp.where(low, o[:TM], o[TM:]).astype(BF16)


def _attn_a(sink, qa, ka, va, lc):
    B, S, W = qa.shape
    nblk = S // TM
    blk = pl.BlockSpec((1, TM, W), lambda b, j: (b, j, 0))
    full = pl.BlockSpec((1, S, W), lambda b, j: (b, 0, 0))
    return pl.pallas_call(
        functools.partial(_attn_a_kernel, lc=lc),
        grid=(B, nblk),
        in_specs=[pl.BlockSpec(memory_space=pltpu.SMEM), blk, full, full],
        out_specs=blk,
        out_shape=jax.ShapeDtypeStruct((B, S, W), BF16),
        compiler_params=_cparams(("parallel", "parallel"), 4 * S * W * 2 + 12 * 2 * TM * 768 * 4),
        name="attn_a",
    )(sink, qa, ka, va)


def _split3(x):
    hi = x.astype(BF16)
    r = x - hi.astype(F32)
    mid = r.astype(BF16)
    lo = (r - mid.astype(F32)).astype(BF16)
    return hi, mid, lo


def _dot_f32_by_01(x, e01):
    return _dot(jnp.concatenate(_split3(x), axis=1), jnp.concatenate([e01] * 3, axis=0))


def _dot_01_by_f32(t01, x):
    n = x.shape[1]
    r = _dot(t01, jnp.concatenate(_split3(x), axis=1))
    return r[:, :n] + r[:, n:2 * n] + r[:, 2 * n:]


KEY_CHUNK = 512


def _attn_c_body(q_ref, k_ref, v_ref, cl_ref, sw_ref, o_ref, s_ref, chunks, lambda_init):
    cl = cl_ref[...]
    lam = (jnp.exp(jnp.sum(cl[0:1] * cl[1:2], axis=-1, keepdims=True))
           - jnp.exp(jnp.sum(cl[2:3] * cl[3:4], axis=-1, keepdims=True)) + lambda_init)
    lane = _lane_iota((TM, LANES))
    seg_r = _row_iota((LANES, LANES)) // C_V_DIM
    seg_c = _lane_iota((LANES, LANES)) // C_V_DIM
    headsum = (seg_r == seg_c).astype(BF16)
    lane2 = _lane_iota((2 * TM, LANES))
    mrun, m, acc, qs, outs = None, None, None, None, []
    for t in range(C_HEADS + 1):
        if t >= 1:
            php, phh, pslot = (t - 1) // 2, (t - 1) % 2, (t - 1) % 2
            pcs = slice(php * LANES, (php + 1) * LANES)
            m = jnp.max(mrun, axis=-1, keepdims=True)
            acc = jnp.zeros((2 * TM, LANES), F32)
        if t < C_HEADS:
            hp, hh, slot = t // 2, t % 2, t % 2
            cs = slice(hp * LANES, (hp + 1) * LANES)
            q = q_ref[0, :, cs]
            zero = jnp.zeros_like(q)
            base = hh * 2 * C_QK_DIM
            in0 = (lane >= base) & (lane < base + C_QK_DIM)
            in1 = (lane >= base + C_QK_DIM) & (lane < base + 2 * C_QK_DIM)
            qs = jnp.concatenate([jnp.where(in0, q, zero), jnp.where(in1, q, zero)], axis=0)
            mrun = jnp.full((2 * TM, LANES), NEG, F32)
        for st, sz in chunks:
            if t < C_HEADS:
                s = _dot_nt(qs, k_ref[0, st:st + sz, cs])
                s_ref[slot, :, st:st + sz] = s
                for u in range(sz // LANES):
                    mrun = jnp.maximum(mrun, s[:, u * LANES:(u + 1) * LANES])
            if t >= 1:
                e = jnp.exp(s_ref[pslot, :, st:st + sz] - m).astype(BF16)
                vs = v_ref[0, st:st + sz, pcs]
                own = (_lane_iota(vs.shape) // C_V_DIM) == phh
                acc = acc + _dot(e, jnp.where(own, vs, jnp.ones_like(vs)))
        if t >= 1:
            own_o = (lane2 // C_V_DIM) == phh
            on = acc / jnp.where(own_o, pltpu.roll(acc, C_V_DIM, axis=1), 1.0)
            outs.append(on[:TM] - lam * on[TM:])
            if phh == 1:
                o = jnp.where(lane < C_V_DIM, outs[0], outs[1])
                ss = _dot_f32_by_01(o * o, headsum)
                y = o * lax.rsqrt(ss * (1.0 / C_V_DIM) + EPS) * sw_ref[...] * (1.0 - lambda_init)
                o_ref[0, :, pcs] = y.astype(BF16)
                outs = []


def _attn_c_kernel(q_ref, k_ref, v_ref, cl_ref, sw_ref, o_ref, s_ref, *, lc, lambda_init):
    j = pl.program_id(1)
    S = k_ref.shape[1]
    ctx_chunks = [(0, lc)]
    all_chunks = ctx_chunks + [(st, KEY_CHUNK) for st in range(lc, S, KEY_CHUNK)]

    @pl.when(j == 0)
    def _():
        _attn_c_body(q_ref, k_ref, v_ref, cl_ref, sw_ref, o_ref, s_ref, ctx_chunks, lambda_init)

    @pl.when(j > 0)
    def _():
        _attn_c_body(q_ref, k_ref, v_ref, cl_ref, sw_ref, o_ref, s_ref, all_chunks, lambda_init)


def _attn_c(qc, kc, vc, c_lambda, subln_w, lc, lambda_init):
    B, S, W = qc.shape
    assert lc == TM and (S - lc) % KEY_CHUNK == 0
    blk = pl.BlockSpec((1, TM, W), lambda b, j: (b, j, 0))
    full = pl.BlockSpec((1, S, W), lambda b, j: (b, 0, 0))
    sw = jnp.tile(subln_w, LANES // C_V_DIM).reshape(1, LANES)
    return pl.pallas_call(
        functools.partial(_attn_c_kernel, lc=lc, lambda_init=lambda_init),
        grid=(B, S // TM),
        in_specs=[blk, full, full, pl.BlockSpec((4, C_QK_DIM), lambda b, j: (0, 0)),
                  pl.BlockSpec((1, LANES), lambda b, j: (0, 0))],
        out_specs=blk,
        out_shape=jax.ShapeDtypeStruct((B, S, W), BF16),
        scratch_shapes=[pltpu.VMEM((2, 2 * TM, S), F32)],
        compiler_params=_cparams(("parallel", "parallel"),
                                 4 * S * W * 2 + 2 * 2 * TM * S * 4 + 6 * 2 * TM * KEY_CHUNK * 4),
        name="attn_c",
    )(qc, kc, vc, c_lambda, sw)


XBC_W = B_D_INNER + 2 * B_GROUPS * B_STATE
XB_W = XBC_W + LANES


def _ssd_prep(prev_ref, cur_ref, next_ref, cw_ref, cb_ref, dtb_ref, alog_ref, c, ncc, nch):
    cur = cur_ref[0]
    first = (c == 0) | (c == ncc)
    last = (c == ncc - 1) | (c == nch - 1)
    prev = jnp.where(first, 0.0, prev_ref[0][:, :XBC_W])
    nxt = jnp.where(last, 0.0, next_ref[0][:, :XBC_W])
    ext = jnp.concatenate([prev, cur[:, :XBC_W], nxt], axis=0)
    rows = CH + 2 * SUBLANES
    acc = jnp.zeros((CH, XBC_W), F32) + cb_ref[...]
    for k in range(B_CONV):
        sh = (B_CONV // 2 - k) % rows
        r = ext if sh == 0 else pltpu.roll(ext, sh, axis=0)
        acc = acc + r[SUBLANES:SUBLANES + CH] * cw_ref[k:k + 1, :]
    u = _silu(acc)
    xs = u[:, :B_D_INNER]
    bm = u[:, B_D_INNER:B_D_INNER + LANES]
    cm = u[:, B_D_INNER + LANES:]
    xdt_raw = cur[:, XBC_W:] + dtb_ref[...]
    dt = jnp.maximum(xdt_raw, 0.0) + jnp.log(1.0 + jnp.exp(-jnp.abs(xdt_raw)))
    dta = dt * (-jnp.exp(alog_ref[...]))
    li = _row_iota((CH, CH))
    si = _lane_iota((CH, CH))
    tl = (si <= li).astype(BF16)
    tu = (si >= li).astype(BF16)
    lane = _lane_iota((CH, LANES))
    cum = jnp.where(lane < B_HEADS, _dot_01_by_f32(tl, dta), _dot_01_by_f32(tu, dta))
    er = _row_iota((LANES, B_D_INNER))
    ec = _lane_iota((LANES, B_D_INNER)) // B_HEADDIM
    dtx, cumx = [], []
    for d in range(2):
        e = (er == ec + d * B_HEADS).astype(BF16)
        dtx.append(_dot_f32_by_01(dt, e))
        cumx.append(_dot_f32_by_01(cum, e))
    aend = [cumx[0][CH - 1:CH, :], cumx[1][0:1, :]]
    return xs, bm, cm, cum, dtx, cumx, aend


def _ssd_specs(S, B):
    n8 = S // SUBLANES
    per = CH // SUBLANES
    prev = pl.BlockSpec((1, SUBLANES, XB_W), lambda b, c: (b, jnp.maximum(c * per - 1, 0), 0))
    cur = pl.BlockSpec((1, CH, XB_W), lambda b, c: (b, c, 0))
    nxt = pl.BlockSpec((1, SUBLANES, XB_W), lambda b, c: (b, jnp.minimum(c * per + per, n8 - 1), 0))
    const = lambda r, w: pl.BlockSpec((r, w), lambda b, c: (0, 0))
    return [prev, cur, nxt, const(B_CONV, XBC_W), const(1, XBC_W), const(1, LANES), const(1, LANES)]


def _ssd_state_kernel(prev_ref, cur_ref, next_ref, cw_ref, cb_ref, dtb_ref, alog_ref, s_ref, da_ref,
                      *, ncc, nch):
    c = pl.program_id(1)
    xs, bm, cm, cum, dtx, cumx, aend = _ssd_prep(prev_ref, cur_ref, next_ref, cw_ref, cb_ref, dtb_ref,
                                                 alog_ref, c, ncc, nch)
    bmt = bm.T.astype(BF16)
    lane = _lane_iota((B_STATE, B_D_INNER))
    for d in range(2):
        xdec = (xs * dtx[d] * jnp.exp(aend[d] - cumx[d])).astype(BF16)
        s2 = _dot(bmt, xdec)
        s_ref[0, 0, d] = jnp.where(lane < B_D_INNER // 2, s2[:B_STATE], s2[B_STATE:])
        da_ref[0, 0, d] = jnp.broadcast_to(jnp.exp(aend[d]), (SUBLANES, B_D_INNER))


def _ssd_scan_kernel(s_ref, da_ref, h_ref, *, ncc, nch):
    order_f = list(range(nch))
    order_b = list(range(ncc - 1, -1, -1)) + list(range(nch - 1, ncc - 1, -1))
    for d, order in enumerate((order_f, order_b)):
        h = jnp.zeros((B_STATE, B_D_INNER), F32)
        for c in order:
            h_ref[0, c, d] = h
            h = da_ref[0, c, d, 0:1, :] * h + s_ref[0, c, d]


def _ssd_out_kernel(prev_ref, cur_ref, next_ref, cw_ref, cb_ref, dtb_ref, alog_ref, z_ref, h_ref, dsk_ref,
                    nw_ref, o_ref, *, ncc, nch):
    c = pl.program_id(1)
    xs, bm, cm, cum, dtx, cumx, aend = _ssd_prep(prev_ref, cur_ref, next_ref, cw_ref, cb_ref, dtb_ref,
                                                 alog_ref, c, ncc, nch)
    half = B_D_INNER // 2
    lane_s = _lane_iota((CH, LANES))
    lane_h = _lane_iota((B_STATE, B_D_INNER))
    lane_y = _lane_iota((CH, half)) // B_HEADDIM
    cmb = cm.astype(BF16)
    bmb = bm.astype(BF16)
    zero = jnp.zeros_like(cmb)
    gmat = [_dot_nt(jnp.where((lane_s // B_STATE) == g, cmb, zero), bmb) for g in range(B_GROUPS)]
    cumt = cum.T
    li = _row_iota((CH, CH))
    si = _lane_iota((CH, CH))
    y = xs * (dsk_ref[0:1, :] + dsk_ref[1:2, :])
    for d in range(2):
        tri = (si <= li) if d == 0 else (si >= li)
        hc = h_ref[0, 0, d]
        h2 = jnp.concatenate([jnp.where(lane_h < half, hc, 0.0), jnp.where(lane_h < half, 0.0, hc)], axis=0)
        y = y + _dot(cmb, h2.astype(BF16)) * jnp.exp(cumx[d])
        xdt = (xs * dtx[d]).astype(BF16)
        parts = []
        for g in range(B_GROUPS):
            acc = jnp.zeros((CH, half), F32)
            for hl in range(B_HEADS // B_GROUPS):
                idx = d * B_HEADS + g * (B_HEADS // B_GROUPS) + hl
                seg = cum[:, idx:idx + 1] - cumt[idx:idx + 1, :]
                dec = jnp.where(tri, jnp.exp(jnp.where(tri, seg, 0.0)), 0.0)
                sc = (gmat[g] * dec).astype(BF16)
                yh = _dot(sc, xdt[:, g * half:(g + 1) * half])
                acc = jnp.where(lane_y == hl, yh, acc)
            parts.append(acc)
        y = y + jnp.concatenate(parts, axis=1)
    gz = y * _silu(z_ref[0])
    outs = []
    for g in range(B_GROUPS):
        gg = gz[:, g * half:(g + 1) * half]
        ms = jnp.mean(gg * gg, axis=-1, keepdims=True)
        outs.append(gg * lax.rsqrt(ms + EPS))
    o_ref[0] = (jnp.concatenate(outs, axis=1) * nw_ref[...]).astype(BF16)


def _ssd(z, xb, conv_w, conv_b, dt_bias, a_log, d_skip, norm_w, lc):
    B, S, _ = z.shape
    nch, ncc = S // CH, lc // CH
    pad = lambda v: jnp.pad(v.reshape(1, -1), ((0, 0), (0, LANES - v.size)))
    consts = [conv_w, conv_b.reshape(1, XBC_W), pad(dt_bias), pad(a_log)]
    specs = _ssd_specs(S, B)
    st_spec = pl.BlockSpec((1, 1, 2, B_STATE, B_D_INNER), lambda b, c: (b, c, 0, 0, 0))
    da_spec = pl.BlockSpec((1, 1, 2, SUBLANES, B_D_INNER), lambda b, c: (b, c, 0, 0, 0))
    st_shape = jax.ShapeDtypeStruct((B, nch, 2, B_STATE, B_D_INNER), F32)
    da_shape = jax.ShapeDtypeStruct((B, nch, 2, SUBLANES, B_D_INNER), F32)
    small = 16 * CH * XB_W * 4
    states, da = pl.pallas_call(
        functools.partial(_ssd_state_kernel, ncc=ncc, nch=nch),
        grid=(B, nch), in_specs=specs, out_specs=[st_spec, da_spec], out_shape=[st_shape, da_shape],
        compiler_params=_cparams(("parallel", "parallel"), small),
        name="ssd_state",
    )(xb, xb, xb, *consts)
    st_bytes = nch * 2 * B_STATE * B_D_INNER * 4
    hstart = pl.pallas_call(
        functools.partial(_ssd_scan_kernel, ncc=ncc, nch=nch),
        grid=(B,),
        in_specs=[pl.BlockSpec((1, nch, 2, B_STATE, B_D_INNER), lambda b: (b, 0, 0, 0, 0)),
                  pl.BlockSpec((1, nch, 2, SUBLANES, B_D_INNER), lambda b: (b, 0, 0, 0, 0))],
        out_specs=pl.BlockSpec((1, nch, 2, B_STATE, B_D_INNER), lambda b: (b, 0, 0, 0, 0)),
        out_shape=st_shape,
        compiler_params=_cparams(("parallel",), 5 * st_bytes // 2),
        name="ssd_scan",
    )(states, da)
    dsk = jnp.repeat(d_skip, B_HEADDIM, axis=1)
    return pl.pallas_call(
        functools.partial(_ssd_out_kernel, ncc=ncc, nch=nch),
        grid=(B, nch),
        in_specs=specs + [pl.BlockSpec((1, CH, B_D_INNER), lambda b, c: (b, c, 0)), st_spec,
                          pl.BlockSpec((2, B_D_INNER), lambda b, c: (0, 0)),
                          pl.BlockSpec((1, B_D_INNER), lambda b, c: (0, 0))],
        out_specs=pl.BlockSpec((1, CH, B_D_INNER), lambda b, c: (b, c, 0)),
        out_shape=jax.ShapeDtypeStruct((B, S, B_D_INNER), BF16),
        compiler_params=_cparams(("parallel", "parallel"), small),
        name="ssd_out",
    )(xb, xb, xb, *consts, z, hstart, dsk, norm_w.reshape(1, B_D_INNER))


def _outproj_kernel(oa_ref, ob_ref, oc_ref, x_ref, mod_ref, n2_ref, w_ref, wr_ref, xo_ref, h2_ref, rt_ref,
                    cnt_ref):
    @pl.when((pl.program_id(0) == 0) & (pl.program_id(1) == 0))
    def _():
        cnt_ref[...] = jnp.zeros_like(cnt_ref)

    a_w = A_HEADS * A_HEAD_DIM
    mix = (_dot(oa_ref[0], w_ref[0:a_w, :]) + _dot(ob_ref[0], w_ref[a_w:a_w + B_D_INNER, :])
           + _dot(oc_ref[0], w_ref[a_w + B_D_INNER:, :]))
    x = x_ref[0] + mod_ref[0, 2:3, :] * mix
    xo_ref[0] = x
    ms = jnp.mean(x * x, axis=-1, keepdims=True)
    h2 = x * lax.rsqrt(ms + EPS) * n2_ref[...]
    h2 = h2 * (1.0 + mod_ref[0, 4:5, :]) + mod_ref[0, 3:4, :]
    h2_ref[0] = h2
    h_hi, h_mid, _ = _split3(h2)
    logit = _dot(jnp.concatenate([h_hi, h_hi, h_mid], axis=1), wr_ref[...])
    lf = _lane_iota(logit.shape).astype(F32)
    big = 1e9
    gmask = lf < N_GROUPS
    gl = jnp.where(gmask, logit, NEG)
    gm = jnp.max(gl, axis=-1, keepdims=True)
    g_p = 1.0 / jnp.sum(jnp.exp(gl - gm), axis=-1, keepdims=True)
    g_sel = jnp.min(jnp.where(gmask & (gl == gm), lf, big), axis=-1, keepdims=True)
    lo = N_GROUPS + EXPERTS_PER_GROUP * g_sel
    emask = (lf >= lo) & (lf < lo + EXPERTS_PER_GROUP)
    el = jnp.where(emask, logit, NEG)
    v1 = jnp.max(el, axis=-1, keepdims=True)
    i1 = jnp.min(jnp.where(emask & (el == v1), lf, big), axis=-1, keepdims=True)
    rest = emask & (lf != i1)
    el2 = jnp.where(rest, logit, NEG)
    v2 = jnp.max(el2, axis=-1, keepdims=True)
    i2 = jnp.min(jnp.where(rest & (el2 == v2), lf, big), axis=-1, keepdims=True)
    t = jnp.exp(v2 - v1)
    gate1 = g_p / (1.0 + t)
    gate2 = g_p * t / (1.0 + t)
    oh1 = lf == i1 - N_GROUPS
    oh2 = lf == i2 - N_GROUPS
    oh = jnp.where(oh1 | oh2, 1.0, 0.0)
    earlier = (_lane_iota((TM, TM)) < _row_iota((TM, TM))).astype(BF16)
    prefix = _dot(earlier, oh.astype(BF16)) + cnt_ref[0:1, :]
    rank1 = jnp.sum(jnp.where(oh1, prefix, 0.0), axis=-1, keepdims=True)
    rank2 = jnp.sum(jnp.where(oh2, prefix, 0.0), axis=-1, keepdims=True)
    cnt_ref[...] = cnt_ref[...] + jnp.sum(oh, axis=0, keepdims=True)
    info = jnp.zeros_like(logit)
    for k, v in enumerate((i1 - N_GROUPS, i2 - N_GROUPS, gate1, gate2, rank1, rank2)):
        info = jnp.where(lf == k, v, info)
    rt_ref[0] = info


RT_E, RT_GATE, RT_RANK = 0, 2, 4


def _outproj(oa, ob, oc, x, mod_l, n2w, w_out, w_r):
    B, S, D = x.shape
    row = lambda w: pl.BlockSpec((1, TM, w), lambda b, j: (b, j, 0))
    mixw = w_out.shape[0]
    return pl.pallas_call(
        _outproj_kernel,
        grid=(B, S // TM),
        in_specs=[row(oa.shape[2]), row(ob.shape[2]), row(oc.shape[2]), row(D),
                  pl.BlockSpec((1, 6, D), _mod_index(B)), pl.BlockSpec((1, D), lambda b, j: (0, 0)),
                  pl.BlockSpec((mixw, D), lambda b, j: (0, 0)),
                  pl.BlockSpec((3 * D, LANES), lambda b, j: (0, 0))],
        out_specs=[row(D), row(D), row(LANES), pl.BlockSpec((SUBLANES, LANES), lambda b, j: (0, 0))],
        out_shape=[jax.ShapeDtypeStruct((B, S, D), F32), jax.ShapeDtypeStruct((B, S, D), F32),
                   jax.ShapeDtypeStruct((B, S, LANES), F32), jax.ShapeDtypeStruct((SUBLANES, LANES), F32)],
        compiler_params=_cparams(("arbitrary", "arbitrary"), 2 * mixw * D * 2 + 14 * TM * D * 4),
        name="outproj",
    )(oa, ob, oc, x, mod_l, n2w.reshape(1, D), w_out, w_r)


def _expert_kernel(be_ref, nu_ref, nxt_ref, slot_ref, x_ref, wg_hbm, wu_hbm, wd_hbm, o_ref,
                   wgf_ref, wuf_ref, wdf_ref, wgb_ref, wub_ref, wdb_ref, sem, *, layer):
    i = pl.program_id(0)
    used = i < nu_ref[0]

    def weight_copies(e, s):
        return [pltpu.make_async_copy(src.at[layer, e], dst.at[s], sem.at[s, n])
                for n, (src, dst) in enumerate(((wg_hbm, wgf_ref), (wu_hbm, wuf_ref), (wd_hbm, wdf_ref)))]

    @pl.when(i == 0)
    def _():
        for cp in weight_copies(be_ref[0], 0):
            cp.start()

    @pl.when(used & (nxt_ref[i] >= -1))
    def _():
        s = slot_ref[i]
        for cp in weight_copies(be_ref[i], s):
            cp.wait()

        @pl.when(nxt_ref[i] >= 0)
        def _():
            for cp in weight_copies(nxt_ref[i], 1 - s):
                cp.start()

        wgb_ref[...] = wgf_ref[s].astype(BF16)
        wub_ref[...] = wuf_ref[s].astype(BF16)
        wdb_ref[...] = wdf_ref[s].astype(BF16)

    @pl.when(used)
    def _():
        xb = x_ref[...].astype(BF16)
        hid = _silu(_dot(xb, wgb_ref[...])) * _dot(xb, wub_ref[...])
        o_ref[...] = _dot(hid.astype(BF16), wdb_ref[...])

    @pl.when(jnp.logical_not(used))
    def _():
        o_ref[...] = jnp.zeros_like(o_ref)


def _expert_mlp(buf, block_e, n_used, w_gate, w_up, w_down, layer):
    rows, D = buf.shape
    de = w_gate.shape[3]
    nblk = rows // MOE_ROWS
    idx = jnp.arange(nblk)
    first = (idx < n_used[0]) & ((idx == 0) | (block_e != jnp.roll(block_e, 1)))
    slot = ((jnp.cumsum(first) - 1) % 2).astype(jnp.int32)
    first_pos = jnp.where(first, idx, nblk)
    after = jnp.concatenate([lax.cummin(first_pos, reverse=True)[1:], jnp.full((1,), nblk)])
    nxt_e = jnp.sum(jnp.where(after[:, None] == idx[None, :], block_e[None, :] + 1, 0), axis=1) - 1
    nxt = jnp.where(first, nxt_e, -2).astype(jnp.int32)
    hbm = pl.BlockSpec(memory_space=pl.ANY)
    grid_spec = pltpu.PrefetchScalarGridSpec(
        num_scalar_prefetch=4, grid=(nblk,),
        in_specs=[pl.BlockSpec((MOE_ROWS, D), lambda i, be, nu, nx, sl: (jnp.minimum(i, nu[0] - 1), 0)),
                  hbm, hbm, hbm],
        out_specs=pl.BlockSpec((MOE_ROWS, D), lambda i, be, nu, nx, sl: (i, 0)),
        scratch_shapes=[pltpu.VMEM((2, D, de), F32), pltpu.VMEM((2, D, de), F32), pltpu.VMEM((2, de, D), F32),
                        pltpu.VMEM((D, de), BF16), pltpu.VMEM((D, de), BF16), pltpu.VMEM((de, D), BF16),
                        pltpu.SemaphoreType.DMA((2, 3))])
    return pl.pallas_call(
        functools.partial(_expert_kernel, layer=layer), grid_spec=grid_spec,
        out_shape=jax.ShapeDtypeStruct((rows, D), F32),
        compiler_params=_cparams(("arbitrary",), 2 * 3 * D * de * 4 + 3 * D * de * 2 + 8 * MOE_ROWS * D * 4),
        name="expert_mlp",
    )(block_e, n_used, nxt, slot, buf, w_gate, w_up, w_down)


def _dispatch_plan(route, counts):
    T = route.shape[0]
    n_blocks = -(-T * TOP_K // MOE_ROWS) + N_EXPERTS
    cnt = counts[0, :N_EXPERTS].astype(jnp.int32)
    nb = (cnt + MOE_ROWS - 1) // MOE_ROWS
    blk_end = jnp.cumsum(nb)
    row_start = (blk_end - nb) * MOE_ROWS
    e = route[:, RT_E:RT_E + TOP_K].astype(jnp.int32)
    rank = route[:, RT_RANK:RT_RANK + TOP_K].astype(jnp.int32)
    table = jnp.where(e[:, :, None] == jnp.arange(N_EXPERTS)[None, None, :], row_start[None, None, :], 0)
    pos = (jnp.sum(table, axis=-1) + rank).reshape(T // TM, 1, TM * TOP_K)
    block_e = jnp.sum(blk_end[None, :] <= jnp.arange(n_blocks)[:, None], axis=1)
    block_e = jnp.minimum(block_e, N_EXPERTS - 1).astype(jnp.int32)
    n_used = blk_end[-1].astype(jnp.int32).reshape(1)
    tails = jnp.concatenate([row_start + cnt, nb * MOE_ROWS - cnt, n_used]).astype(jnp.int32)
    return pos.astype(jnp.int32), block_e, n_used, n_blocks, tails


def _zero_tails(tails_ref, zero_ref, buf_ref, sem, op):
    for e in range(N_EXPERTS):
        start, length = tails_ref[e], tails_ref[N_EXPERTS + e]
        end = start + length
        bit = MOE_ROWS // 2
        while bit >= SUBLANES:
            done = (length // (2 * bit)) * (2 * bit)

            @pl.when((length & bit) != 0)
            def _(bit=bit, done=done, end=end):
                at = pl.multiple_of(end - done - bit, SUBLANES)
                cp = pltpu.make_async_copy(zero_ref.at[pl.ds(0, bit)], buf_ref.at[pl.ds(at, bit)], sem)
                cp.start() if op == "start" else cp.wait()

            bit //= 2
        for r in range(SUBLANES - 1):
            @pl.when(r < (length & (SUBLANES - 1)))
            def _(r=r, start=start):
                cp = _row_copy(zero_ref, 0, buf_ref, start + r, sem)
                cp.start() if op == "start" else cp.wait()
    half = zero_ref.shape[0]
    for blk in range(buf_ref.shape[0] // MOE_ROWS):
        @pl.when(blk >= tails_ref[2 * N_EXPERTS])
        def _(blk=blk):
            for at in range(blk * MOE_ROWS, (blk + 1) * MOE_ROWS, half):
                cp = pltpu.make_async_copy(zero_ref, buf_ref.at[pl.ds(at, half)], sem)
                cp.start() if op == "start" else cp.wait()


def _dispatch_kernel(tails_ref, pos_ref, x_ref, buf_ref, zero_ref, sem, zsem):
    @pl.when(pl.program_id(0) == 0)
    def _():
        zero_ref[...] = jnp.zeros_like(zero_ref)
        _zero_tails(tails_ref, zero_ref, buf_ref, zsem, "start")
        _zero_tails(tails_ref, zero_ref, buf_ref, zsem, "wait")

    for op in ("start", "wait"):
        for r in range(TM):
            for k in range(TOP_K):
                cp = _row_copy(x_ref, r, buf_ref, pos_ref[0, 0, TOP_K * r + k], sem)
                cp.start(priority=k) if op == "start" else cp.wait()


def _dispatch(h2, pos, tails, n_blocks):
    T, D = h2.shape
    grid_spec = pltpu.PrefetchScalarGridSpec(
        num_scalar_prefetch=1, grid=(T // TM,),
        in_specs=[pl.BlockSpec((1, 1, TM * TOP_K), lambda i, tl: (i, 0, 0), memory_space=pltpu.SMEM),
                  pl.BlockSpec((TM, D), lambda i, tl: (i, 0))],
        out_specs=pl.BlockSpec(memory_space=pl.ANY),
        scratch_shapes=[pltpu.VMEM((MOE_ROWS // 2, D), F32), pltpu.SemaphoreType.DMA, pltpu.SemaphoreType.DMA])
    return pl.pallas_call(
        _dispatch_kernel, grid_spec=grid_spec,
        out_shape=jax.ShapeDtypeStruct((n_blocks * MOE_ROWS, D), F32),
        compiler_params=_cparams(("arbitrary",), 6 * TM * D * 4),
        name="moe_dispatch",
    )(tails, pos, h2)


def _final_kernel(pos_ref, posn_ref, x_ref, rt_ref, eo_ref, modp_ref, nw_ref, o_ref, rows_ref, sem):
    slot = pl.program_id(0) % 2

    @pl.when(pl.program_id(0) == 0)
    def _():
        _gather_rows(eo_ref, pos_ref, rows_ref.at[0], sem.at[0], "start")
        _gather_rows(eo_ref, pos_ref, rows_ref.at[0], sem.at[0], "wait")

    _gather_rows(eo_ref, posn_ref, rows_ref.at[1 - slot], sem.at[1 - slot], "start")
    rt = rt_ref[0]
    y = rt[:, RT_GATE:RT_GATE + 1] * rows_ref[slot, 0] + rt[:, RT_GATE + 1:RT_GATE + 2] * rows_ref[slot, 1]
    x = x_ref[0] + modp_ref[0, 5:6, :] * y
    ms = jnp.mean(x * x, axis=-1, keepdims=True)
    o_ref[0] = x * lax.rsqrt(ms + EPS) * nw_ref[...]
    _gather_rows(eo_ref, posn_ref, rows_ref.at[1 - slot], sem.at[1 - slot], "wait")


def _final(x, moe, mod_prev, nw, lc):
    eo, pos, route = moe
    B, S, D = x.shape
    off, nblk = lc // TM, S // TM
    nlat = nblk - off
    nsteps = B * nlat
    blk = lambda i: (i // nlat) * nblk + i % nlat + off
    row = lambda w: pl.BlockSpec((1, TM, w), lambda i: (i // nlat, i % nlat + off, 0))
    posspec = lambda f: pl.BlockSpec((1, 1, TM * TOP_K), lambda i: (f(i), 0, 0), memory_space=pltpu.SMEM)
    return pl.pallas_call(
        _final_kernel,
        grid=(nsteps,),
        in_specs=[posspec(blk), posspec(lambda i: blk(jnp.minimum(i + 1, nsteps - 1))), row(D), row(LANES),
                  pl.BlockSpec(memory_space=pl.ANY), pl.BlockSpec((1, 6, D), lambda i: (i // nlat, 0, 0)),
                  pl.BlockSpec((1, D), lambda i: (0, 0))],
        out_specs=pl.BlockSpec((1, TM, D), lambda i: (i // nlat, i % nlat, 0)),
        out_shape=jax.ShapeDtypeStruct((B, S - lc, D), F32),
        scratch_shapes=[pltpu.VMEM((2, TOP_K, TM, D), F32), pltpu.SemaphoreType.DMA((2,))],
        compiler_params=_cparams(("arbitrary",), 10 * TM * D * 4),
        name="final_norm",
    )(pos, pos, x, route.reshape(B, S, LANES), eo, mod_prev, nw.reshape(1, D))


def _pack_w_in(w):
    d = w.shape[0]
    aq, ak, av = w[:, 0:256], w[:, 256:384], w[:, 384:512]
    dup = lambda m: jnp.concatenate([m[:, 0:64], m[:, 0:64], m[:, 64:128], m[:, 64:128]], axis=1)
    bz, bx, bdt = w[:, 512:1024], w[:, 1024:1792], w[:, 1792:1808]
    cq, ck, cv = w[:, 1808:2064], w[:, 2064:2320], w[:, 2320:2576]
    packed = jnp.concatenate([aq, dup(ak), dup(av), bz, bx, bdt, jnp.zeros((d, LANES - 16), w.dtype),
                              cq, ck, cv], axis=1)
    return packed.astype(BF16)


def kernel(x, c, ctx, c_ctx, w_mod, b_mod, norm1_w, norm2_w, w_in, w_out, a_sink, b_conv_w, b_conv_b,
           b_dt_bias, b_a_log, b_d, b_norm_w, c_lambda, c_subln_w, moe_group_router, moe_router,
           moe_w_gate, moe_w_up, moe_w_down, final_norm_w):
    B, L, D = x.shape
    Lc = ctx.shape[1]
    depth = w_mod.shape[0]
    assert Lc == TM and L % TM == 0 and B + 1 <= SUBLANES
    S = Lc + L
    T = B * S
    xa = jnp.concatenate([ctx, x], axis=1)
    c_all = jnp.concatenate([c, c_ctx[None, :], jnp.zeros((SUBLANES - B - 1, D), F32)], axis=0)
    mod = _mod_table(c_all, w_mod, b_mod)
    tabs = _rope_tables(L, Lc, A_HEAD_DIM) + _rope_tables(L, Lc, C_QK_DIM)
    moe, mod_prev = None, None
    for l in range(depth):
        mod_l = mod[l].reshape(SUBLANES, 6, D)
        lambda_init = 0.8 - 0.6 * math.exp(-0.3 * l)
        outs = _inproj(xa, moe, mod_prev, mod_l, norm1_w[l], _pack_w_in(w_in[l]), tabs)
        if moe is not None:
            xa, outs = outs[0], outs[1:]
        qa, ka, va, z, xb, qc, kc, vc = outs
        oa = _attn_a(a_sink[l], qa, ka, va, Lc)
        ob = _ssd(z, xb, b_conv_w[l], b_conv_b[l], b_dt_bias[l], b_a_log[l], b_d[l], b_norm_w[l], Lc)
        oc = _attn_c(qc, kc, vc, c_lambda[l], c_subln_w[l], Lc, lambda_init)
        w_r = jnp.concatenate([moe_group_router[l], moe_router[l],
                               jnp.zeros((D, LANES - N_GROUPS - N_EXPERTS), F32)], axis=1)
        w_hi = w_r.astype(BF16)
        w_mid = (w_r - w_hi.astype(F32)).astype(BF16)
        xa, h2, route, counts = _outproj(oa, ob, oc, xa, mod_l, norm2_w[l], w_out[l].astype(BF16),
                                         jnp.concatenate([w_hi, w_mid, w_hi], axis=0))
        route = route.reshape(T, LANES)
        pos, block_e, n_used, n_blocks, tails = _dispatch_plan(route, counts)
        buf = _dispatch(h2.reshape(T, D), pos, tails, n_blocks)
        eo = _expert_mlp(buf, block_e, n_used, moe_w_gate, moe_w_up, moe_w_down, l)
        moe, mod_prev = (eo, pos, route), mod_l
    return _final(xa, moe, mod_prev, final_norm_w, Lc)
```
